```python
import math
import jax
import jax.numpy as jnp
from jax import lax
import numpy as np

D_MODEL = 2048
BATCH = 1
SEQ = 16384
DEPTH = 4
DEC_BATCH = 2
DEC_SEQ = 8192
PAST_LEN = 128

HEAD_DIM = 128
MIX_HEADS = 12
MIX_DIM = MIX_HEADS * HEAD_DIM
MEM_HEADS = 4
MEM_DIM = MEM_HEADS * HEAD_DIM
N_MEM = 256
GRID_W = 64
WIN_R = 8
WIN_C = 16
CONV_W = 5
CHUNK = 64
N_EXPERTS = 16
EXPERT_FF = 2048
CAPACITY_FACTOR = 2
N_NA = (DEPTH + 1) // 2
N_GDN = DEPTH // 2
NA_IN = 3 * MIX_DIM + MEM_DIM
GDN_IN = 4 * MIX_DIM + 4 * MIX_HEADS + MEM_DIM
EPS = 1e-6

kernel_name = "hybrid_natten_gdn_ec_moe_encoder"


def rmsnorm(x, gain):
    xf = x.astype(jnp.float32)
    y = xf * lax.rsqrt(jnp.mean(xf * xf, axis=-1, keepdims=True) + EPS)
    return (y * gain.astype(jnp.float32)).astype(x.dtype)


def l2norm(x):
    return x * lax.rsqrt(jnp.sum(x * x, axis=-1, keepdims=True) + EPS)


def neighbourhood_attention(q, k, v, rpb):
    B, n, H, dh = q.shape
    rows = n // GRID_W
    wr = min(WIN_R, rows)
    qg = q.reshape(B, rows, GRID_W, H, dh)
    kg = k.reshape(B, rows, GRID_W, H, dh)
    vg = v.reshape(B, rows, GRID_W, H, dh)
    col = jnp.arange(GRID_W)
    col_start = jnp.clip(col - WIN_C // 2, 0, GRID_W - WIN_C)
    col_idx = col_start[:, None] + jnp.arange(WIN_C)[None, :]
    dc = col_idx - col[:, None] + (WIN_C - 1)
    scale = dh ** -0.5

    def row_block(r):
        rs = jnp.clip(r - wr // 2, 0, rows - wr)
        kb = lax.dynamic_slice_in_dim(kg, rs, wr, axis=1)[:, :, col_idx]
        vb = lax.dynamic_slice_in_dim(vg, rs, wr, axis=1)[:, :, col_idx]
        qb = lax.dynamic_index_in_dim(qg, r, axis=1, keepdims=False)
        dr = rs + jnp.arange(wr) - r + (WIN_R - 1)
        bias = rpb[:, dr[None, :, None], dc[:, None, :]]
        s = jnp.einsum('bqhd,brqkhd->bhqrk', qb, kb).astype(jnp.float32) * scale + bias[None].astype(jnp.float32)
        p = jax.nn.softmax(s.reshape(B, H, GRID_W, wr * WIN_C), axis=-1)
        p = p.reshape(B, H, GRID_W, wr, WIN_C).astype(v.dtype)
        return jnp.einsum('bhqrk,brqkhd->bqhd', p, vb)

    out = lax.map(row_block, jnp.arange(rows))
    return jnp.moveaxis(out, 0, 1).reshape(B, n, H, dh)


def memory_attention(q, k, v):
    s = jnp.einsum('bnhd,bmhd->bhnm', q, k).astype(jnp.float32) * (q.shape[-1] ** -0.5)
    p = jax.nn.softmax(s, axis=-1).astype(v.dtype)
    return jnp.einsum('bhnm,bmhd->bnhd', p, v)


def centred_depthwise_conv(x, w):
    c = x.shape[-1]
    pad = CONV_W // 2
    return lax.conv_general_dilated(x, w[:, None, :].astype(x.dtype), window_strides=(1,),
                                    padding=[(pad, pad)], dimension_numbers=('NWC', 'WIO', 'NWC'),
                                    feature_group_count=c)


def gated_delta_rule_chunked(q, k, v, g, beta):
    B, T, H, dk = q.shape
    dv = v.shape[-1]
    nc = T // CHUNK

    def chunks(t):
        return t.reshape(B, nc, CHUNK, H, -1).transpose(0, 3, 1, 2, 4)

    qc, kc, vc = chunks(q), chunks(k), chunks(v)
    gc = jnp.cumsum(chunks(g[..., None])[..., 0], axis=-1)
    bc = chunks(beta[..., None])[..., 0]
    incl = jnp.tril(jnp.ones((CHUNK, CHUNK), bool))
    strict = jnp.tril(jnp.ones((CHUNK, CHUNK), bool), -1)
    diff = gc[..., :, None] - gc[..., None, :]
    decay = jnp.where(incl, jnp.exp(jnp.where(incl, diff, 0.0)), 0.0)
    kk = jnp.einsum('bhncd,bhnsd->bhncs', kc, kc)
    a_mat = jnp.where(strict, bc[..., :, None] * kk * decay, 0.0) + jnp.eye(CHUNK, dtype=q.dtype)
    rhs = jnp.concatenate([vc * bc[..., None], kc * (bc * jnp.exp(gc))[..., None]], axis=-1)
    sol = lax.linalg.triangular_solve(a_mat, rhs, left_side=True, lower=True)
    u, w = sol[..., :dv], sol[..., dv:]
    qk = jnp.einsum('bhncd,bhnsd->bhncs', qc, kc) * decay
    g_last = gc[..., -1]
    q_dec = qc * jnp.exp(gc)[..., None]
    k_dec = kc * jnp.exp(g_last[..., None] - gc)[..., None]

    def step(S, inp):
        u_i, w_i, q_i, k_i, qk_i, gl_i = inp
        v_new = u_i - jnp.einsum('bhcd,bhde->bhce', w_i, S)
        o_i = jnp.einsum('bhcd,bhde->bhce', q_i, S) + jnp.einsum('bhcs,bhse->bhce', qk_i, v_new)
        S = S * jnp.exp(gl_i)[..., None, None] + jnp.einsum('bhcd,bhce->bhde', k_i, v_new)
        return S, o_i

    xs = (jnp.moveaxis(u, 2, 0), jnp.moveaxis(w, 2, 0), jnp.moveaxis(q_dec, 2, 0),
          jnp.moveaxis(k_dec, 2, 0), jnp.moveaxis(qk, 2, 0), jnp.moveaxis(g_last, 2, 0))
    S0 = jnp.zeros((B, H, dk, dv), q.dtype)
    _, o = lax.scan(step, S0, xs)
    return o.transpose(1, 0, 3, 2, 4).reshape(B, T, H, dv)


def bidirectional_gated_deltanet(qkv, z, b, a, conv_w, a_log, dt_bias, out_gain):
    f32 = jnp.float32
    B, n, _ = qkv.shape
    qkv = jax.nn.silu(centred_depthwise_conv(qkv, conv_w)).astype(f32)
    q, k, v = [t.reshape(B, n, MIX_HEADS, HEAD_DIM) for t in jnp.split(qkv, 3, axis=-1)]
    q = l2norm(q) * (HEAD_DIM ** -0.5)
    k = l2norm(k)
    beta = jax.nn.sigmoid(b.astype(f32)).reshape(B, n, 2, MIX_HEADS)
    g = -jnp.exp(a_log.astype(f32)) * jax.nn.softplus(a.astype(f32).reshape(B, n, 2, MIX_HEADS) + dt_bias.astype(f32))
    o_fwd = gated_delta_rule_chunked(q, k, v, g[:, :, 0], beta[:, :, 0])
    flip = lambda t: jnp.flip(t, axis=1)
    o_bwd = flip(gated_delta_rule_chunked(flip(q), flip(k), flip(v), flip(g[:, :, 1]), flip(beta[:, :, 1])))
    o = rmsnorm(o_fwd + o_bwd, out_gain) * jax.nn.silu(z.astype(f32).reshape(B, n, MIX_HEADS, HEAD_DIM))
    return o.astype(z.dtype)


def expert_choice_ffn(h, w_router, w_gate_up, w_down):
    B, n, D = h.shape
    N = B * n
    t = h.reshape(N, D)
    aff = jax.nn.softmax((t @ w_router).astype(jnp.float32), axis=-1)
    cap = CAPACITY_FACTOR * N // N_EXPERTS
    gate, idx = lax.top_k(aff.T, cap)
    xe = t[idx]
    gt, up = jnp.split(jnp.einsum('ecd,edf->ecf', xe, w_gate_up), 2, axis=-1)
    ye = jnp.einsum('ecf,efd->ecd', jax.nn.silu(gt) * up, w_down) * gate[..., None].astype(h.dtype)
    out = jnp.zeros((N, D), h.dtype).at[idx.reshape(-1)].add(ye.reshape(-1, D))
    return out.reshape(B, n, D)


def encoder_trunk(x, mem, p):
    B, n, _ = x.shape
    m = mem.shape[1]
    for i in range(DEPTH):
        j = i // 2
        h = rmsnorm(x, p['norm_mix'][i])
        if i % 2 == 0:
            proj = h @ p['na_w_in'][j]
            q, k, v, xq = jnp.split(proj, [MIX_DIM, 2 * MIX_DIM, 3 * MIX_DIM], axis=-1)
            heads = lambda t: t.reshape(B, n, MIX_HEADS, HEAD_DIM)
            mix = neighbourhood_attention(heads(q), heads(k), heads(v), p['na_rpb'][j])
        else:
            proj = h @ p['gdn_w_in'][j]
            qkv, z, bb, aa, xq = jnp.split(
                proj, [3 * MIX_DIM, 4 * MIX_DIM, 4 * MIX_DIM + 2 * MIX_HEADS, 4 * MIX_DIM + 4 * MIX_HEADS], axis=-1)
            mix = bidirectional_gated_deltanet(qkv, z, bb, aa, p['gdn_conv'][j], p['gdn_a_log'][j],
                                               p['gdn_dt_bias'][j], p['gdn_out_norm'][j])
        mkv = rmsnorm(mem, p['norm_mem'][i]) @ p['w_mem_kv'][i]
        mk, mv = [t.reshape(B, m, MEM_HEADS, HEAD_DIM) for t in jnp.split(mkv, 2, axis=-1)]
        xo = memory_attention(xq.reshape(B, n, MEM_HEADS, HEAD_DIM), mk, mv)
        o = jnp.concatenate([mix.reshape(B, n, MIX_DIM), xo.reshape(B, n, MEM_DIM)], axis=-1) @ p['w_out'][i]
        x = x + o
        x = x + expert_choice_ffn(rmsnorm(x, p['norm_ffn'][i]), p['w_router'][i], p['w_gate_up'][i], p['w_down'][i])
    return rmsnorm(x, p['norm_final'])


def setup_inputs(seed: int = 0) -> dict:
    key = jax.random.key(seed)
    ks = jax.random.split(key, 24)
    f32 = jnp.float32
    nrm = lambda k, shape, scale: jax.random.normal(k, shape, f32) * scale
    gain = lambda k, shape: 1.0 + 0.05 * jax.random.normal(k, shape, f32)
    dt = jnp.exp(jax.random.uniform(ks[10], (N_GDN, 2, MIX_HEADS), f32, math.log(1e-3), math.log(1e-1)))
    return {
        'x_prompt': nrm(ks[0], (BATCH, SEQ, D_MODEL), 1.0),
        'x_sample': nrm(ks[1], (DEC_BATCH, DEC_SEQ, D_MODEL), 1.0),
        'mem_prompt': nrm(ks[2], (BATCH, N_MEM, D_MODEL), 1.0),
        'mem_sample': nrm(ks[3], (DEC_BATCH, N_MEM, D_MODEL), 1.0),
        'norm_mix': gain(ks[4], (DEPTH, D_MODEL)),
        'na_w_in': nrm(ks[5], (N_NA, D_MODEL, NA_IN), D_MODEL ** -0.5),
        'na_rpb': nrm(ks[6], (N_NA, MIX_HEADS, 2 * WIN_R - 1, 2 * WIN_C - 1), 0.1),
        'gdn_w_in': nrm(ks[7], (N_GDN, D_MODEL, GDN_IN), D_MODEL ** -0.5),
        'gdn_conv': nrm(ks[8], (N_GDN, CONV_W, 3 * MIX_DIM), CONV_W ** -0.5),
        'gdn_a_log': jnp.log(jax.random.uniform(ks[9], (N_GDN, 2, MIX_HEADS), f32, 1.0, 16.0)),
        'gdn_dt_bias': dt + jnp.log(-jnp.expm1(-dt)),
        'gdn_out_norm': gain(ks[11], (N_GDN, HEAD_DIM)),
        'norm_mem': gain(ks[12], (DEPTH, D_MODEL)),
        'w_mem_kv': nrm(ks[13], (DEPTH, D_MODEL, 2 * MEM_DIM), D_MODEL ** -0.5),
        'w_out': nrm(ks[14], (DEPTH, MIX_DIM + MEM_DIM, D_MODEL), (MIX_DIM + MEM_DIM) ** -0.5),
        'norm_ffn': gain(ks[15], (DEPTH, D_MODEL)),
        'w_router': nrm(ks[16], (DEPTH, D_MODEL, N_EXPERTS), D_MODEL ** -0.5),
        'w_gate_up': nrm(ks[17], (DEPTH, N_EXPERTS, D_MODEL, 2 * EXPERT_FF), D_MODEL ** -0.5),
        'w_down': nrm(ks[18], (DEPTH, N_EXPERTS, EXPERT_FF, D_MODEL), EXPERT_FF ** -0.5),
        'norm_final': gain(ks[19], (D_MODEL,)),
    }


def reference(x_prompt, x_sample, mem_prompt, mem_sample, norm_mix, na_w_in, na_rpb, gdn_w_in, gdn_conv,
              gdn_a_log, gdn_dt_bias, gdn_out_norm, norm_mem, w_mem_kv, w_out, norm_ffn, w_router,
              w_gate_up, w_down, norm_final):
    p = {'norm_mix': norm_mix, 'na_w_in': na_w_in, 'na_rpb': na_rpb, 'gdn_w_in': gdn_w_in,
         'gdn_conv': gdn_conv, 'gdn_a_log': gdn_a_log, 'gdn_dt_bias': gdn_dt_bias,
         'gdn_out_norm': gdn_out_norm, 'norm_mem': norm_mem, 'w_mem_kv': w_mem_kv, 'w_out': w_out,
         'norm_ffn': norm_ffn, 'w_router': w_router, 'w_gate_up': w_gate_up, 'w_down': w_down,
         'norm_final': norm_final}
    y_prompt = encoder_trunk(x_prompt, mem_prompt, p)
    y_sample = encoder_trunk(x_sample, mem_sample, p)
    return (y_prompt, y_sample)
```

```python
import functools
import math

import jax
import jax.numpy as jnp
import numpy as np
from jax import lax
from jax.experimental import pallas as pl
from jax.experimental.pallas import tpu as pltpu

F32 = jnp.float32
BF16 = jnp.bfloat16

HEAD_DIM = 128
MIX_HEADS = 12
MIX_DIM = MIX_HEADS * HEAD_DIM
MEM_HEADS = 4
MEM_DIM = MEM_HEADS * HEAD_DIM
GRID_W = 64
WIN_R = 8
WIN_C = 16
CONV_W = 5
CHUNK = 64
N_EXPERTS = 16
CAPACITY_FACTOR = 2
EPS = 1e-6
NEG_BIG = -1e30

V7X_VMEM_LIMIT_BYTES = 56 * 1024 * 1024
GDN_GROUP = 4 * CHUNK
GDN_HEADS_PER_STEP = 2


def _params(*sem):
    return pltpu.CompilerParams(dimension_semantics=sem, vmem_limit_bytes=V7X_VMEM_LIMIT_BYTES)


def _dot(a, b):
    return jnp.dot(a, b, preferred_element_type=F32)


def _dot_nt(a, b):
    return lax.dot_general(a, b, (((1,), (1,)), ((), ())), preferred_element_type=F32)


def _dot_tn(a, b):
    return lax.dot_general(a, b, (((0,), (0,)), ((), ())), preferred_element_type=F32)


def _rms_matmul_kernel(x_ref, g_ref, w_ref, o_ref, xn_ref, *, tm, rc):
    @pl.when(pl.program_id(1) == 0)
    def _():
        def body(c, carry):
            r = pl.multiple_of(c * rc, rc)
            x = x_ref[pl.ds(r, rc), :]
            ms = jnp.mean(x * x, axis=-1, keepdims=True)
            xn_ref[pl.ds(r, rc), :] = (x * lax.rsqrt(ms + EPS) * g_ref[...]).astype(BF16)
            return carry
        lax.fori_loop(0, tm // rc, body, 0)

    o_ref[...] = _dot(xn_ref[...], w_ref[...]).astype(o_ref.dtype)


def rms_matmul(x, gain, w, out_dtype, tm=1024, tn=512):
    M, D = x.shape
    N = w.shape[1]
    tm = min(tm, M)
    tn = min(tn, N)
    assert M % tm == 0 and N % tn == 0
    rc = min(128, tm)
    return pl.pallas_call(
        functools.partial(_rms_matmul_kernel, tm=tm, rc=rc),
        grid=(M // tm, N // tn),
        in_specs=[
            pl.BlockSpec((tm, D), lambda i, j: (i, 0)),
            pl.BlockSpec((1, D), lambda i, j: (0, 0)),
            pl.BlockSpec((D, tn), lambda i, j: (0, j)),
        ],
        out_specs=pl.BlockSpec((tm, tn), lambda i, j: (i, j)),
        out_shape=jax.ShapeDtypeStruct((M, N), out_dtype),
        scratch_shapes=[pltpu.VMEM((tm, D), BF16)],
        compiler_params=_params("parallel", "arbitrary"),
        name="rms_matmul",
    )(x, gain.reshape(1, D), w)


def _na_bias_table(rpb):
    col = np.arange(GRID_W)
    cs = np.clip(col - WIN_C // 2, 0, GRID_W - WIN_C)
    c2 = np.arange(GRID_W)
    inside = (c2[None, :] >= cs[:, None]) & (c2[None, :] < cs[:, None] + WIN_C)
    dc = np.clip(c2[None, :] - col[:, None] + (WIN_C - 1), 0, 2 * WIN_C - 2)
    t = np.arange(WIN_R)
    i = np.arange(WIN_R)
    dr = t[:, None] + i[None, :]
    tab = rpb[:, dr[:, :, None, None], dc[None, None, :, :]]
    tab = jnp.where(jnp.asarray(inside)[None, None, None], tab.astype(F32), NEG_BIG)
    tab = jnp.transpose(tab, (0, 1, 3, 2, 4))
    return tab.reshape(rpb.shape[0], WIN_R, GRID_W, WIN_R * GRID_W)


def _na_kernel(q_ref, k_ref, v_ref, b_ref, o_ref, *, rows_prompt, rows_sample):
    s = pl.program_id(1)
    rows = jnp.where(s == 0, rows_prompt, rows_sample)
    base = jnp.where(s == 2, rows_sample, 0)
    scale = HEAD_DIM ** -0.5
    kw = WIN_R * GRID_W

    def body(r, carry):
        rs = jnp.clip(r - WIN_R // 2, 0, rows - WIN_R)
        t = rs - r + (WIN_R - 1)
        q = q_ref[pl.ds(pl.multiple_of((base + r) * GRID_W, GRID_W), GRID_W), :]
        k0 = pl.multiple_of((base + rs) * GRID_W, GRID_W)
        k = k_ref[pl.ds(k0, kw), :]
        v = v_ref[pl.ds(k0, kw), :]
        sc = _dot_nt(q, k) * scale + b_ref[t]
        m = jnp.max(sc, axis=-1, keepdims=True)
        p = jnp.exp(sc - m)
        l = jnp.sum(p, axis=-1, keepdims=True)
        o = _dot(p.astype(BF16), v) / l
        o_ref[pl.ds(pl.multiple_of((base + r) * GRID_W, GRID_W), GRID_W), :] = o.astype(o_ref.dtype)
        return carry

    lax.fori_loop(0, rows, body, 0)


def neighbourhood_attention(proj, bias_tab, n_tok):
    M = proj.shape[0]
    assert M == 2 * n_tok
    rows_prompt = n_tok // GRID_W
    rows_sample = rows_prompt // 2
    assert rows_sample >= WIN_R
    blk = lambda off: pl.BlockSpec((n_tok, HEAD_DIM), lambda h, s: (jnp.minimum(s, 1), h + off))
    return pl.pallas_call(
        functools.partial(_na_kernel, rows_prompt=rows_prompt, rows_sample=rows_sample),
        grid=(MIX_HEADS, 3),
        in_specs=[
            blk(0), blk(MIX_HEADS), blk(2 * MIX_HEADS),
            pl.BlockSpec((None, WIN_R, GRID_W, WIN_R * GRID_W), lambda h, s: (h, 0, 0, 0)),
        ],
        out_specs=blk(0),
        out_shape=jax.ShapeDtypeStruct((M, MIX_DIM), BF16),
        compiler_params=_params("parallel", "arbitrary"),
        name="neighbourhood_attention",
    )(proj, proj, proj, bias_tab)


def _softplus(x):
    return jnp.maximum(x, 0.0) + jnp.log1p(jnp.exp(-jnp.abs(x)))


def _gdn_direction(q, k, v, kk, qk, g_row, beta_row, s_state, reverse):
    G = q.shape[0]
    nchunk = G // CHUNK
    ri = lax.broadcasted_iota(jnp.int32, (G, G), 0)
    ci = lax.broadcasted_iota(jnp.int32, (G, G), 1)
    shift = int(math.log2(CHUNK))
    same = jnp.right_shift(ri, shift) == jnp.right_shift(ci, shift)
    if reverse:
        incl = same & (ci >= ri)
        strict = same & (ci > ri)
    else:
        incl = same & (ci <= ri)
        strict = same & (ci < ri)
    eye = ri == ci

    gc = jnp.sum(jnp.where(incl, g_row, 0.0), axis=1, keepdims=True)
    gl = jnp.sum(jnp.where(same, g_row, 0.0), axis=1, keepdims=True)
    beta = jnp.sum(jnp.where(eye, beta_row, 0.0), axis=1, keepdims=True)
    cm = jnp.broadcast_to(gc, (G, G))
    diff = cm - cm.T
    decay = jnp.where(incl, jnp.exp(jnp.where(incl, diff, 0.0)), 0.0)
    a_neg = jnp.where(strict, -(beta * kk * decay), 0.0)

    tm = jnp.where(eye, 1.0, 0.0) + a_neg
    p = a_neg
    for _ in range(int(math.log2(CHUNK)) - 1):
        pb = p.astype(BF16)
        p = _dot(pb, pb)
        tm = tm + _dot(tm.astype(BF16), p.astype(BF16))

    egc = jnp.exp(gc)
    kf = k.astype(F32)
    rhs = jnp.concatenate([v.astype(F32) * beta, kf * (beta * egc)], axis=1)
    uw = _dot(tm.astype(BF16), rhs.astype(BF16))
    dh = q.shape[1]
    u = uw[:, :dh]
    w = uw[:, dh:].astype(BF16)
    qkd = (qk * decay).astype(BF16)
    q_dec = (q.astype(F32) * egc).astype(BF16)
    k_dec = (kf * jnp.exp(gl - gc)).astype(BF16)
    egl = jnp.exp(gl)

    outs = [None] * nchunk
    order = range(nchunk - 1, -1, -1) if reverse else range(nchunk)
    for c in order:
        lo, hi = c * CHUNK, (c + 1) * CHUNK
        sb = s_state.astype(BF16)
        v_new = u[lo:hi] - _dot(w[lo:hi], sb)
        vb = v_new.astype(BF16)
        outs[c] = _dot(q_dec[lo:hi], sb) + _dot(qkd[lo:hi, lo:hi], vb)
        s_state = s_state * egl[lo:lo + 1, :] + _dot_tn(k_dec[lo:hi], vb)
    return jnp.concatenate(outs, axis=0), s_state


def _gdn_kernel(qf_ref, kf_ref, vf_ref, af_ref, bf_ref, qb_ref, kb_ref, vb_ref, ab_ref, bb_ref,
                alog_ref, dtb_ref, of_ref, ob_ref, s_ref, *, steps_prompt, steps_sample):
    hb = pl.program_id(0)
    t = pl.program_id(1)
    first = (t == 0) | (t == steps_prompt) | (t == steps_prompt + steps_sample)

    @pl.when(first)
    def _():
        s_ref[...] = jnp.zeros_like(s_ref)

    for d, (q_ref, k_ref, v_ref, a_ref, b_ref, o_ref) in enumerate(
            ((qf_ref, kf_ref, vf_ref, af_ref, bf_ref, of_ref), (qb_ref, kb_ref, vb_ref, ab_ref, bb_ref, ob_ref))):
        for j in range(GDN_HEADS_PER_STEP):
            lanes = slice(j * HEAD_DIM, (j + 1) * HEAD_DIM)
            q = q_ref[:, lanes]
            k = k_ref[:, lanes]
            v = v_ref[:, lanes]
            gate_row = d * MIX_HEADS + hb * GDN_HEADS_PER_STEP + j
            a_row = a_ref[pl.ds(gate_row, 1), :]
            b_row = b_ref[pl.ds(gate_row, 1), :]
            g_row = -jnp.exp(alog_ref[pl.ds(gate_row, 1), :]) * _softplus(a_row + dtb_ref[pl.ds(gate_row, 1), :])
            beta_row = jax.nn.sigmoid(b_row)
            kk = _dot_nt(k, k)
            qk = _dot_nt(q, k)
            o, s_new = _gdn_direction(q, k, v, kk, qk, g_row, beta_row, s_ref[d, j], reverse=(d == 1))
            s_ref[d, j] = s_new
            o_ref[:, lanes] = o


def gated_delta(q, k, v, a_t, b_t, a_log, dt_bias, n_tok):
    M = q.shape[0]
    assert M == 2 * n_tok
    steps_prompt = n_tok // GDN_GROUP
    steps_sample = steps_prompt // 2
    assert steps_sample >= 1 and n_tok % (2 * GDN_GROUP) == 0
    n_steps = 2 * steps_prompt
    hw = GDN_HEADS_PER_STEP * HEAD_DIM

    def bwd_block(t):
        in_prompt = t < steps_prompt
        in_s1 = t < steps_prompt + steps_sample
        start = jnp.where(in_prompt, 0, jnp.where(in_s1, steps_prompt, steps_prompt + steps_sample))
        length = jnp.where(in_prompt, steps_prompt, steps_sample)
        return start + length - 1 - (t - start)

    tok_f = pl.BlockSpec((GDN_GROUP, hw), lambda h, t: (t, h))
    tok_b = pl.BlockSpec((GDN_GROUP, hw), lambda h, t: (bwd_block(t), h))
    gate_f = pl.BlockSpec((2 * MIX_HEADS, GDN_GROUP), lambda h, t: (0, t))
    gate_b = pl.BlockSpec((2 * MIX_HEADS, GDN_GROUP), lambda h, t: (0, bwd_block(t)))
    small = pl.BlockSpec((2 * MIX_HEADS, 1), lambda h, t: (0, 0))
    out = jax.ShapeDtypeStruct((M, MIX_DIM), F32)
    return pl.pallas_call(
        functools.partial(_gdn_kernel, steps_prompt=steps_prompt, steps_sample=steps_sample),
        grid=(MIX_HEADS // GDN_HEADS_PER_STEP, n_steps),
        in_specs=[tok_f, tok_f, tok_f, gate_f, gate_f, tok_b, tok_b, tok_b, gate_b, gate_b, small, small],
        out_specs=[tok_f, tok_b],
        out_shape=[out, out],
        scratch_shapes=[pltpu.VMEM((2, GDN_HEADS_PER_STEP, HEAD_DIM, HEAD_DIM), F32)],
        compiler_params=_params("parallel", "arbitrary"),
        name="gated_delta",
    )(q, k, v, a_t, b_t, q, k, v, a_t, b_t,
      a_log.reshape(2 * MIX_HEADS, 1).astype(F32), dt_bias.reshape(2 * MIX_HEADS, 1).astype(F32))


def _post_mixer_kernel(*refs, gdn):
    if gdn:
        (of_ref, ob_ref, z_ref, og_ref, xq_ref, mkv_ref, wo_ref, x_ref, gf_ref, wr_ref,
         xo_ref, h_ref, aff_ref) = refs
    else:
        (mix_ref, xq_ref, mkv_ref, wo_ref, x_ref, gf_ref, wr_ref, xo_ref, h_ref, aff_ref) = refs
    scale = HEAD_DIM ** -0.5

    parts = []
    if gdn:
        for h in range(MIX_HEADS):
            lanes = slice(h * HEAD_DIM, (h + 1) * HEAD_DIM)
            o = of_ref[:, lanes] + ob_ref[:, lanes]
            o = o * lax.rsqrt(jnp.mean(o * o, axis=-1, keepdims=True) + EPS) * og_ref[...]
            z = z_ref[:, lanes]
            parts.append((o * (z * jax.nn.sigmoid(z))).astype(BF16))
    else:
        parts.append(mix_ref[...])

    for h in range(MEM_HEADS):
        q = xq_ref[:, h * HEAD_DIM:(h + 1) * HEAD_DIM].astype(BF16)
        k = mkv_ref[:, h * HEAD_DIM:(h + 1) * HEAD_DIM]
        v = mkv_ref[:, MEM_DIM + h * HEAD_DIM:MEM_DIM + (h + 1) * HEAD_DIM]
        sc = _dot_nt(q, k) * scale
        p = jnp.exp(sc - jnp.max(sc, axis=-1, keepdims=True))
        l = jnp.sum(p, axis=-1, keepdims=True)
        parts.append((_dot(p.astype(BF16), v) / l).astype(BF16))

    lhs = jnp.concatenate(parts, axis=1)
    x = x_ref[...] + _dot(lhs, wo_ref[...])
    xo_ref[...] = x
    hn = x * lax.rsqrt(jnp.mean(x * x, axis=-1, keepdims=True) + EPS) * gf_ref[...]
    h_ref[...] = hn.astype(BF16)
    logits = lax.dot_general(wr_ref[...], hn, (((1,), (1,)), ((), ())), preferred_element_type=F32,
                             precision=lax.Precision.HIGHEST)
    e = jnp.exp(logits - jnp.max(logits, axis=0, keepdims=True))
    aff_ref[...] = e / jnp.sum(e, axis=0, keepdims=True)


def post_mixer(mix_inputs, proj, xq_col_block, mkv, w_out, x, gain_ffn, w_router_t, n_tok, gdn, tm=512):
    M, D = x.shape
    tm = min(tm, n_tok // 2)
    assert n_tok % (2 * tm) == 0
    blocks_prompt = n_tok // tm
    blocks_sample = blocks_prompt // 2
    n_mem = mkv.shape[1]

    def mem_batch(i):
        return jnp.where(i < blocks_prompt, 0, 1 + (i - blocks_prompt) // blocks_sample)

    row = lambda width, cb=0: pl.BlockSpec((tm, width), lambda i: (i, cb))
    full = lambda a: pl.BlockSpec(a.shape, lambda i: (0,) * a.ndim)
    if gdn:
        o_f, o_b, out_gain = mix_inputs
        og = out_gain.reshape(1, HEAD_DIM).astype(F32)
        ins = [o_f, o_b, proj, og]
        specs = [row(MIX_DIM), row(MIX_DIM), row(MIX_DIM, 3), full(og)]
    else:
        ins = [mix_inputs[0]]
        specs = [row(MIX_DIM)]
    gf = gain_ffn.reshape(1, D).astype(F32)
    ins += [proj, mkv, w_out, x, gf, w_router_t]
    specs += [
        row(MEM_DIM, xq_col_block),
        pl.BlockSpec((None, n_mem, 2 * MEM_DIM), lambda i: (mem_batch(i), 0, 0)),
        full(w_out), row(D), full(gf), full(w_router_t),
    ]
    return pl.pallas_call(
        functools.partial(_post_mixer_kernel, gdn=gdn),
        grid=(M // tm,),
        in_specs=specs,
        out_specs=[row(D), row(D), pl.BlockSpec((N_EXPERTS, tm), lambda i: (0, i))],
        out_shape=[jax.ShapeDtypeStruct((M, D), F32), jax.ShapeDtypeStruct((M, D), BF16),
                   jax.ShapeDtypeStruct((N_EXPERTS, M), F32)],
        compiler_params=_params("parallel"),
        name="post_mixer_gdn" if gdn else "post_mixer_na",
    )(*ins)


def _expert_ffn_kernel(x_ref, gate_ref, wg_ref, wu_ref, wd_ref, o_ref, h_ref, wgb_ref, wub_ref, wdb_ref,
                       *, n_f, tf, rc):
    s = pl.program_id(1)
    cap = x_ref.shape[0]

    @pl.when(s < n_f)
    def _():
        wgb_ref[...] = wg_ref[...].astype(BF16)
        wub_ref[...] = wu_ref[...].astype(BF16)

        def body(c, carry):
            r = pl.multiple_of(c * rc, rc)
            xc = x_ref[pl.ds(r, rc), :]
            g = _dot(xc, wgb_ref[...])
            u = _dot(xc, wub_ref[...])
            h_ref[s, pl.ds(r, rc), :] = (g * jax.nn.sigmoid(g) * u).astype(BF16)
            return carry
        lax.fori_loop(0, cap // rc, body, 0)

    @pl.when(s >= n_f)
    def _():
        wdb_ref[...] = wd_ref[...].astype(BF16)

        def body(c, carry):
            r = pl.multiple_of(c * rc, rc)
            y = _dot(h_ref[0, pl.ds(r, rc), :], wdb_ref[0:tf, :])
            for f in range(1, n_f):
                y = y + _dot(h_ref[f, pl.ds(r, rc), :], wdb_ref[f * tf:(f + 1) * tf, :])
            o_ref[pl.ds(r, rc), :] = y * gate_ref[pl.ds(r, rc), :]
            return carry
        lax.fori_loop(0, cap // rc, body, 0)


def expert_ffn(xe, gate, w_gate_up, w_down, tf=256, td=256):
    G, cap, D = xe.shape
    E, _, F2 = w_gate_up.shape
    F = F2 // 2
    tf = min(tf, F)
    td = min(td, D)
    n_f = F // tf
    n_d = D // td
    rc = min(512, cap)
    assert cap % rc == 0
    fcol = lambda s: jnp.minimum(s, n_f - 1)
    dcol = lambda s: jnp.maximum(s - n_f, 0)
    return pl.pallas_call(
        functools.partial(_expert_ffn_kernel, n_f=n_f, tf=tf, rc=rc),
        grid=(G, n_f + n_d),
        in_specs=[
            pl.BlockSpec((None, cap, D), lambda g, s: (g, 0, 0)),
            pl.BlockSpec((None, cap, 1), lambda g, s: (g, 0, 0)),
            pl.BlockSpec((None, D, tf), lambda g, s: (g % E, 0, fcol(s))),
            pl.BlockSpec((None, D, tf), lambda g, s: (g % E, 0, n_f + fcol(s))),
            pl.BlockSpec((None, F, td), lambda g, s: (g % E, 0, dcol(s))),
        ],
        out_specs=pl.BlockSpec((None, cap, td), lambda g, s: (g, 0, dcol(s))),
        out_shape=jax.ShapeDtypeStruct((G, cap, D), F32),
        scratch_shapes=[pltpu.VMEM((n_f, cap, tf), BF16), pltpu.VMEM((D, tf), BF16), pltpu.VMEM((D, tf), BF16),
                        pltpu.VMEM((F, td), BF16)],
        compiler_params=_params("parallel", "arbitrary"),
        name="expert_ffn",
    )(xe, gate, w_gate_up, w_gate_up, w_down)


def _rmsnorm_kernel(x_ref, g_ref, o_ref):
    x = x_ref[...]
    o_ref[...] = x * lax.rsqrt(jnp.mean(x * x, axis=-1, keepdims=True) + EPS) * g_ref[...]


def rmsnorm_rows(x, gain, tm=512):
    M, D = x.shape
    tm = min(tm, M)
    return pl.pallas_call(
        _rmsnorm_kernel,
        grid=(M // tm,),
        in_specs=[pl.BlockSpec((tm, D), lambda i: (i, 0)), pl.BlockSpec((1, D), lambda i: (0, 0))],
        out_specs=pl.BlockSpec((tm, D), lambda i: (i, 0)),
        out_shape=jax.ShapeDtypeStruct((M, D), F32),
        compiler_params=_params("parallel"),
        name="final_rmsnorm",
    )(x, gain.reshape(1, D).astype(F32))


def _gdn_prep(proj, conv_w, seq_lens):
    C = 3 * MIX_DIM
    pad = CONV_W // 2
    outs = []
    start = 0
    for n in seq_lens:
        xs = proj[start:start + n, :C][None]
        y = lax.conv_general_dilated(xs, conv_w[:, None, :].astype(xs.dtype), window_strides=(1,),
                                     padding=[(pad, pad)], dimension_numbers=('NWC', 'WIO', 'NWC'),
                                     feature_group_count=C)[0]
        outs.append(y)
        start += n
    y = jnp.concatenate(outs, axis=0)
    y = jax.nn.silu(y)
    q, k, v = [t.reshape(-1, MIX_HEADS, HEAD_DIM) for t in jnp.split(y, 3, axis=-1)]
    l2 = lambda t: t * lax.rsqrt(jnp.sum(t * t, axis=-1, keepdims=True) + EPS)
    q = l2(q) * (HEAD_DIM ** -0.5)
    k = l2(k)
    flat = lambda t: t.reshape(-1, MIX_DIM).astype(BF16)
    return flat(q), flat(k), flat(v)


def _moe(x, h, aff_t, w_gate_up, w_down, n_tok):
    E = aff_t.shape[0]
    cap = CAPACITY_FACTOR * n_tok // E
    gates, idxs = [], []
    for t in range(2):
        gate, idx = lax.top_k(aff_t[:, t * n_tok:(t + 1) * n_tok], cap)
        gates.append(gate)
        idxs.append(idx + t * n_tok)
    gate = jnp.concatenate(gates, axis=0)
    idx = jnp.concatenate(idxs, axis=0)
    xe = h[idx]
    ye = expert_ffn(xe, gate[..., None], w_gate_up, w_down)
    return x.at[idx.reshape(-1)].add(ye.reshape(-1, x.shape[1]))


def _forward(x_prompt, x_sample, mem_prompt, mem_sample, norm_mix, na_w_in, na_rpb, gdn_w_in, gdn_conv,
             gdn_a_log, gdn_dt_bias, gdn_out_norm, norm_mem, w_mem_kv, w_out, norm_ffn, w_router,
             w_gate_up, w_down, norm_final):
    D = x_prompt.shape[-1]
    n_tok = x_prompt.shape[0] * x_prompt.shape[1]
    assert x_prompt.shape[0] == 1 and x_sample.shape[0] == 2 and x_sample.shape[1] * 2 == n_tok
    n_mem = mem_prompt.shape[1]
    depth = norm_mix.shape[0]
    x = jnp.concatenate([x_prompt.reshape(n_tok, D), x_sample.reshape(n_tok, D)], axis=0)
    mem = jnp.concatenate([mem_prompt.reshape(n_mem, D), mem_sample.reshape(2 * n_mem, D)], axis=0)
    seq_lens = (n_tok, n_tok // 2, n_tok // 2)

    c_qkvz = 4 * MIX_DIM
    n_gate = 2 * MIX_HEADS
    gdn_cols = c_qkvz + MEM_DIM + 2 * n_gate
    gdn_cols_padded = -(-gdn_cols // 512) * 512

    for i in range(depth):
        j = i // 2
        if i % 2 == 0:
            w_in = na_w_in[j].astype(BF16)
            proj = rms_matmul(x, norm_mix[i], w_in, BF16)
            mix = neighbourhood_attention(proj, _na_bias_table(na_rpb[j]), n_tok)
            mix_inputs = (mix,)
            xq_block = 3 * MIX_DIM // MEM_DIM
        else:
            w = gdn_w_in[j]
            w_in = jnp.concatenate(
                [w[:, :c_qkvz], w[:, c_qkvz + 2 * n_gate:], w[:, c_qkvz:c_qkvz + 2 * n_gate],
                 jnp.zeros((D, gdn_cols_padded - gdn_cols), w.dtype)], axis=1).astype(BF16)
            proj = rms_matmul(x, norm_mix[i], w_in, F32)
            q, k, v = _gdn_prep(proj, gdn_conv[j], seq_lens)
            g0 = c_qkvz + MEM_DIM
            b_t = proj[:, g0:g0 + n_gate].T
            a_t = proj[:, g0 + n_gate:g0 + 2 * n_gate].T
            o_f, o_b = gated_delta(q, k, v, a_t, b_t, gdn_a_log[j], gdn_dt_bias[j], n_tok)
            mix_inputs = (o_f, o_b, gdn_out_norm[j])
            xq_block = c_qkvz // MEM_DIM
        mkv = rms_matmul(mem, norm_mem[i], w_mem_kv[i].astype(BF16), BF16, tm=n_mem)
        mkv = mkv.reshape(3, n_mem, 2 * MEM_DIM)
        x, h, aff_t = post_mixer(mix_inputs, proj, xq_block, mkv, w_out[i].astype(BF16), x, norm_ffn[i],
                                 w_router[i].T.astype(F32), n_tok, gdn=(i % 2 == 1))
        x = _moe(x, h, aff_t, w_gate_up[i], w_down[i], n_tok)

    y = rmsnorm_rows(x, norm_final)
    return y[:n_tok].reshape(x_prompt.shape), y[n_tok:].reshape(x_sample.shape)


def kernel(x_prompt, x_sample, mem_prompt, mem_sample, norm_mix, na_w_in, na_rpb, gdn_w_in, gdn_conv, gdn_a_log, gdn_dt_bias, gdn_out_norm, norm_mem, w_mem_kv, w_out, norm_ffn, w_router, w_gate_up, w_down, norm_final):
    return _forward(x_prompt, x_sample, mem_prompt, mem_sample, norm_mix, na_w_in, na_rpb, gdn_w_in, gdn_conv,
                    gdn_a_log, gdn_dt_bias, gdn_out_norm, norm_mem, w_mem_kv, w_out, norm_ffn, w_router,
                    w_gate_up, w_down, norm_final)
```

```python
import functools
import math

import jax
import jax.numpy as jnp
import numpy as np
from jax import lax
from jax.experimental import pallas as pl
from jax.experimental.pallas import tpu as pltpu

F32 = jnp.float32
BF16 = jnp.bfloat16

HEAD_DIM = 128
MIX_HEADS = 12
MIX_DIM = MIX_HEADS * HEAD_DIM
MEM_HEADS = 4
MEM_DIM = MEM_HEADS * HEAD_DIM
GRID_W = 64
WIN_R = 8
WIN_C = 16
CONV_W = 5
CHUNK = 64
N_EXPERTS = 16
CAPACITY_FACTOR = 2
EPS = 1e-6
NEG_BIG = -1e30

V7X_VMEM_LIMIT_BYTES = 56 * 1024 * 1024
GDN_GROUP = 4 * CHUNK
GDN_HEADS_PER_STEP = 2
NA_ROWS_PER_ITER = 4


def _params(*sem):
    return pltpu.CompilerParams(dimension_semantics=sem, vmem_limit_bytes=V7X_VMEM_LIMIT_BYTES)


def _dot(a, b):
    return jnp.dot(a, b, preferred_element_type=F32)


def _dot_nt(a, b):
    return lax.dot_general(a, b, (((1,), (1,)), ((), ())), preferred_element_type=F32)


def _dot_tn(a, b):
    return lax.dot_general(a, b, (((0,), (0,)), ((), ())), preferred_element_type=F32)


def _rms_matmul_kernel(x_ref, g_ref, w_ref, o_ref, xn_ref, *, tm, rc):
    @pl.when(pl.program_id(1) == 0)
    def _():
        def body(c, carry):
            r = pl.multiple_of(c * rc, rc)
            x = x_ref[pl.ds(r, rc), :]
            ms = jnp.mean(x * x, axis=-1, keepdims=True)
            xn_ref[pl.ds(r, rc), :] = (x * lax.rsqrt(ms + EPS) * g_ref[...]).astype(BF16)
            return carry
        lax.fori_loop(0, tm // rc, body, 0)

    o_ref[...] = _dot(xn_ref[...], w_ref[...]).astype(o_ref.dtype)


def rms_matmul(x, gain, w, out_dtype, tm=1024, tn=512):
    M, D = x.shape
    N = w.shape[1]
    tm = min(tm, M)
    tn = min(tn, N)
    assert M % tm == 0 and N % tn == 0
    rc = min(128, tm)
    return pl.pallas_call(
        functools.partial(_rms_matmul_kernel, tm=tm, rc=rc),
        grid=(M // tm, N // tn),
        in_specs=[
            pl.BlockSpec((tm, D), lambda i, j: (i, 0)),
            pl.BlockSpec((1, D), lambda i, j: (0, 0)),
            pl.BlockSpec((D, tn), lambda i, j: (0, j)),
        ],
        out_specs=pl.BlockSpec((tm, tn), lambda i, j: (i, j)),
        out_shape=jax.ShapeDtypeStruct((M, N), out_dtype),
        scratch_shapes=[pltpu.VMEM((tm, D), BF16)],
        compiler_params=_params("parallel", "arbitrary"),
        name="rms_matmul",
    )(x, gain.reshape(1, D), w)


def _na_bias_table(rpb):
    col = np.arange(GRID_W)
    cs = np.clip(col - WIN_C // 2, 0, GRID_W - WIN_C)
    c2 = np.arange(GRID_W)
    inside = (c2[None, :] >= cs[:, None]) & (c2[None, :] < cs[:, None] + WIN_C)
    dc = np.clip(c2[None, :] - col[:, None] + (WIN_C - 1), 0, 2 * WIN_C - 2)
    t = np.arange(WIN_R)
    i = np.arange(WIN_R)
    dr = t[:, None] + i[None, :]
    tab = rpb[:, dr[:, :, None, None], dc[None, None, :, :]]
    tab = jnp.where(jnp.asarray(inside)[None, None, None], tab.astype(F32), NEG_BIG)
    tab = jnp.transpose(tab, (0, 1, 3, 2, 4))
    return tab.reshape(rpb.shape[0], WIN_R, GRID_W, WIN_R * GRID_W)


def _na_kernel(q_ref, k_ref, v_ref, b_ref, o_ref, *, rows_prompt, rows_sample):
    s = pl.program_id(1)
    rows = jnp.where(s == 0, rows_prompt, rows_sample)
    base = jnp.where(s == 2, rows_sample, 0)
    scale = HEAD_DIM ** -0.5
    kw = WIN_R * GRID_W

    def one_row(r):
        rs = jnp.clip(r - WIN_R // 2, 0, rows - WIN_R)
        t = rs - r + (WIN_R - 1)
        q = q_ref[pl.ds(pl.multiple_of((base + r) * GRID_W, GRID_W), GRID_W), :]
        k0 = pl.multiple_of((base + rs) * GRID_W, GRID_W)
        k = k_ref[pl.ds(k0, kw), :]
        v = v_ref[pl.ds(k0, kw), :]
        sc = _dot_nt(q, k) * scale + b_ref[t]
        yield
        m = jnp.max(sc, axis=-1, keepdims=True)
        p = jnp.exp(sc - m)
        l = jnp.sum(p, axis=-1, keepdims=True)
        o = _dot(p.astype(BF16), v) / l
        yield
        o_ref[pl.ds(pl.multiple_of((base + r) * GRID_W, GRID_W), GRID_W), :] = o.astype(o_ref.dtype)

    def body(i, carry):
        active = [one_row(i * NA_ROWS_PER_ITER + u) for u in range(NA_ROWS_PER_ITER)]
        while active:
            active = [g for g in active if next(g, StopIteration) is not StopIteration]
        return carry

    lax.fori_loop(0, rows // NA_ROWS_PER_ITER, body, 0)


def neighbourhood_attention(proj, bias_tab, n_tok):
    M = proj.shape[0]
    assert M == 2 * n_tok
    rows_prompt = n_tok // GRID_W
    rows_sample = rows_prompt // 2
    assert rows_sample >= WIN_R and rows_sample % NA_ROWS_PER_ITER == 0
    blk = lambda off: pl.BlockSpec((n_tok, HEAD_DIM), lambda h, s: (jnp.minimum(s, 1), h + off))
    return pl.pallas_call(
        functools.partial(_na_kernel, rows_prompt=rows_prompt, rows_sample=rows_sample),
        grid=(MIX_HEADS, 3),
        in_specs=[
            blk(0), blk(MIX_HEADS), blk(2 * MIX_HEADS),
            pl.BlockSpec((None, WIN_R, GRID_W, WIN_R * GRID_W), lambda h, s: (h, 0, 0, 0)),
        ],
        out_specs=blk(0),
        out_shape=jax.ShapeDtypeStruct((M, MIX_DIM), BF16),
        compiler_params=_params("parallel", "arbitrary"),
        name="neighbourhood_attention",
    )(proj, proj, proj, bias_tab)


def _softplus(x):
    return jnp.maximum(x, 0.0) + jnp.log1p(jnp.exp(-jnp.abs(x)))


def _gdn_local(q, k, v, g_row, beta_row, reverse, store):
    G = q.shape[0]
    nchunk = G // CHUNK
    kk = _dot_nt(k, k)
    qk = _dot_nt(q, k)
    yield
    ri = lax.broadcasted_iota(jnp.int32, (G, G), 0)
    ci = lax.broadcasted_iota(jnp.int32, (G, G), 1)
    shift = int(math.log2(CHUNK))
    same = jnp.right_shift(ri, shift) == jnp.right_shift(ci, shift)
    if reverse:
        incl = same & (ci >= ri)
        strict = same & (ci > ri)
    else:
        incl = same & (ci <= ri)
        strict = same & (ci < ri)
    eye = ri == ci

    gc = jnp.sum(jnp.where(incl, g_row, 0.0), axis=1, keepdims=True)
    gl = jnp.sum(jnp.where(same, g_row, 0.0), axis=1, keepdims=True)
    beta = jnp.sum(jnp.where(eye, beta_row, 0.0), axis=1, keepdims=True)
    cm = jnp.broadcast_to(gc, (G, G))
    diff = cm - cm.T
    decay = jnp.where(incl, jnp.exp(jnp.where(incl, diff, 0.0)), 0.0)
    a_neg = jnp.where(strict, -(beta * kk * decay), 0.0)

    tm = jnp.where(eye, 1.0, 0.0) + a_neg
    pb = a_neg.astype(BF16)
    p = _dot(pb, pb)
    yield
    for it in range(int(math.log2(CHUNK)) - 1):
        pb = p.astype(BF16)
        tm = tm + _dot(tm.astype(BF16), pb)
        if it < int(math.log2(CHUNK)) - 2:
            p = _dot(pb, pb)
        yield

    egc = jnp.exp(gc)
    kf = k.astype(F32)
    rhs = jnp.concatenate([v.astype(F32) * beta, kf * (beta * egc)], axis=1)
    uw = _dot(tm.astype(BF16), rhs.astype(BF16))
    yield
    dh = q.shape[1]
    u = uw[:, :dh]
    w = uw[:, dh:].astype(BF16)
    qkd = (qk * decay).astype(BF16)
    q_dec = (q.astype(F32) * egc).astype(BF16)
    k_dec = (kf * jnp.exp(gl - gc)).astype(BF16)
    wq = jnp.concatenate(
        [t[c * CHUNK:(c + 1) * CHUNK] for c in range(nchunk) for t in (w, q_dec)], axis=0)
    qkd_blocks = jnp.concatenate(
        [qkd[c * CHUNK:(c + 1) * CHUNK, c * CHUNK:(c + 1) * CHUNK] for c in range(nchunk)], axis=0)
    store(u, wq, k_dec, qkd_blocks, jnp.broadcast_to(jnp.exp(gl), (G, dh)))


def _gdn_scan(u, wq, k_dec, qkd_blocks, egl, s_state, reverse, store):
    G = u.shape[0]
    nchunk = G // CHUNK
    outs = [None] * nchunk
    order = range(nchunk - 1, -1, -1) if reverse else range(nchunk)
    for c in order:
        lo, hi = c * CHUNK, (c + 1) * CHUNK
        r = _dot(wq[2 * lo:2 * hi], s_state.astype(BF16))
        yield
        vb = (u[lo:hi] - r[:CHUNK]).astype(BF16)
        outs[c] = r[CHUNK:] + _dot(qkd_blocks[lo:hi], vb)
        s_state = s_state * egl[lo:lo + 1, :] + _dot_tn(k_dec[lo:hi], vb)
        yield
    store(jnp.concatenate(outs, axis=0), s_state)


def _gdn_kernel(qf_ref, kf_ref, vf_ref, af_ref, bf_ref, qb_ref, kb_ref, vb_ref, ab_ref, bb_ref,
                alog_ref, dtb_ref, of_ref, ob_ref, s_ref, u_ref, wq_ref, kd_ref, qkd_ref, egl_ref,
                *, steps_prompt, steps_sample):
    hb = pl.program_id(0)
    t = pl.program_id(1)
    carried = (s_ref, u_ref, wq_ref, kd_ref, qkd_ref, egl_ref)

    @pl.when(t == 0)
    def _():
        for ref in carried:
            ref[...] = jnp.zeros_like(ref)

    prev = t - 1
    @pl.when((prev == 0) | (prev == steps_prompt) | (prev == steps_prompt + steps_sample))
    def _():
        s_ref[...] = jnp.zeros_like(s_ref)

    scans, locals_ = [], []
    for d, (q_ref, k_ref, v_ref, a_ref, b_ref, o_ref) in enumerate(
            ((qf_ref, kf_ref, vf_ref, af_ref, bf_ref, of_ref), (qb_ref, kb_ref, vb_ref, ab_ref, bb_ref, ob_ref))):
        for j in range(GDN_HEADS_PER_STEP):
            lanes = slice(j * HEAD_DIM, (j + 1) * HEAD_DIM)

            def store_scan(o, s_new, d=d, j=j, o_ref=o_ref, lanes=lanes):
                s_ref[d, j] = s_new
                o_ref[:, lanes] = o

            scans.append(_gdn_scan(u_ref[d, j], wq_ref[d, j], kd_ref[d, j], qkd_ref[d, j], egl_ref[d, j],
                                   s_ref[d, j], reverse=(d == 1), store=store_scan))

            def store_local(u, wq, k_dec, qkd_blocks, egl, d=d, j=j):
                u_ref[d, j] = u
                wq_ref[d, j] = wq
                kd_ref[d, j] = k_dec
                qkd_ref[d, j] = qkd_blocks
                egl_ref[d, j] = egl

            gate_row = d * MIX_HEADS + hb * GDN_HEADS_PER_STEP + j
            a_row = a_ref[pl.ds(gate_row, 1), :]
            b_row = b_ref[pl.ds(gate_row, 1), :]
            g_row = -jnp.exp(alog_ref[pl.ds(gate_row, 1), :]) * _softplus(a_row + dtb_ref[pl.ds(gate_row, 1), :])
            locals_.append(_gdn_local(q_ref[:, lanes], k_ref[:, lanes], v_ref[:, lanes], g_row,
                                      jax.nn.sigmoid(b_row), reverse=(d == 1), store=store_local))

    active = scans + locals_
    while active:
        active = [g for g in active if next(g, StopIteration) is not StopIteration]


def gated_delta(qkv, a_t, b_t, a_log, dt_bias, n_tok):
    M = qkv.shape[0]
    assert M == 2 * n_tok
    steps_prompt = n_tok // GDN_GROUP
    steps_sample = steps_prompt // 2
    assert steps_sample >= 1 and n_tok % (2 * GDN_GROUP) == 0
    n_steps = 2 * steps_prompt
    nh = GDN_HEADS_PER_STEP
    hw = nh * HEAD_DIM
    col_blocks = MIX_DIM // hw

    def bwd_block(t):
        in_prompt = t < steps_prompt
        in_s1 = t < steps_prompt + steps_sample
        start = jnp.where(in_prompt, 0, jnp.where(in_s1, steps_prompt, steps_prompt + steps_sample))
        length = jnp.where(in_prompt, steps_prompt, steps_sample)
        return start + length - 1 - (t - start)

    local = lambda t: jnp.minimum(t, n_steps - 1)
    scan = lambda t: jnp.maximum(t - 1, 0)
    tok_f = lambda off: pl.BlockSpec((GDN_GROUP, hw), lambda h, t: (local(t), h + off * col_blocks))
    tok_b = lambda off: pl.BlockSpec((GDN_GROUP, hw), lambda h, t: (bwd_block(local(t)), h + off * col_blocks))
    gate_f = pl.BlockSpec((2 * MIX_HEADS, GDN_GROUP), lambda h, t: (0, local(t)))
    gate_b = pl.BlockSpec((2 * MIX_HEADS, GDN_GROUP), lambda h, t: (0, bwd_block(local(t))))
    small = pl.BlockSpec((2 * MIX_HEADS, 1), lambda h, t: (0, 0))
    out_f = pl.BlockSpec((GDN_GROUP, hw), lambda h, t: (scan(t), h))
    out_b = pl.BlockSpec((GDN_GROUP, hw), lambda h, t: (bwd_block(scan(t)), h))
    out = jax.ShapeDtypeStruct((M, MIX_DIM), F32)
    per_unit = lambda shape, dtype: pltpu.VMEM((2, nh) + shape, dtype)
    return pl.pallas_call(
        functools.partial(_gdn_kernel, steps_prompt=steps_prompt, steps_sample=steps_sample),
        grid=(MIX_HEADS // nh, n_steps + 1),
        in_specs=[tok_f(0), tok_f(1), tok_f(2), gate_f, gate_f, tok_b(0), tok_b(1), tok_b(2), gate_b, gate_b,
                  small, small],
        out_specs=[out_f, out_b],
        out_shape=[out, out],
        scratch_shapes=[per_unit((HEAD_DIM, HEAD_DIM), F32), per_unit((GDN_GROUP, HEAD_DIM), F32),
                        per_unit((2 * GDN_GROUP, HEAD_DIM), BF16), per_unit((GDN_GROUP, HEAD_DIM), BF16),
                        per_unit((GDN_GROUP, CHUNK), BF16), per_unit((GDN_GROUP, HEAD_DIM), F32)],
        compiler_params=_params("parallel", "arbitrary"),
        name="gated_delta",
    )(qkv, qkv, qkv, a_t, b_t, qkv, qkv, qkv, a_t, b_t,
      a_log.reshape(2 * MIX_HEADS, 1).astype(F32), dt_bias.reshape(2 * MIX_HEADS, 1).astype(F32))


def _gdn_prep_kernel(prev_ref, x_ref, next_ref, w_ref, o_ref, *, blocks_prompt, blocks_sample):
    i = pl.program_id(0)
    part = pl.program_id(1)
    tb = x_ref.shape[0]
    s1 = blocks_prompt
    s2 = blocks_prompt + blocks_sample
    s3 = blocks_prompt + 2 * blocks_sample
    pmask = jnp.where((i == 0) | (i == s1) | (i == s2), 0.0, 1.0)
    nmask = jnp.where((i == s1 - 1) | (i == s2 - 1) | (i == s3 - 1), 0.0, 1.0)
    row = lax.broadcasted_iota(jnp.int32, (tb, HEAD_DIM), 0)
    qscale = jnp.where(part == 0, HEAD_DIM ** -0.5, 1.0)
    for h in range(MIX_HEADS):
        lanes = slice(h * HEAD_DIM, (h + 1) * HEAD_DIM)
        x = x_ref[:, lanes]
        pv = prev_ref[:, lanes] * pmask
        nx = next_ref[:, lanes] * nmask
        w = w_ref[:, lanes]
        xm1 = jnp.where(row == 0, pv[7:8], pltpu.roll(x, 1, axis=0))
        xm2 = jnp.where(row == 0, pv[6:7], jnp.where(row == 1, pv[7:8], pltpu.roll(x, 2, axis=0)))
        xp1 = jnp.where(row == tb - 1, nx[0:1], pltpu.roll(x, tb - 1, axis=0))
        xp2 = jnp.where(row == tb - 2, nx[0:1], jnp.where(row == tb - 1, nx[1:2], pltpu.roll(x, tb - 2, axis=0)))
        y = xm2 * w[0:1] + xm1 * w[1:2] + x * w[2:3] + xp1 * w[3:4] + xp2 * w[4:5]
        y = y * jax.nn.sigmoid(y)
        inv = lax.rsqrt(jnp.sum(y * y, axis=-1, keepdims=True) + EPS) * qscale
        o_ref[:, lanes] = (y * jnp.where(part == 2, 1.0, inv)).astype(o_ref.dtype)


def gdn_prep(proj, conv_w, n_tok, tb=256):
    M = proj.shape[0]
    assert CONV_W == 5 and M == 2 * n_tok
    tb = min(tb, n_tok // 2)
    assert n_tok % (2 * tb) == 0 and tb % 8 == 0
    blocks_prompt = n_tok // tb
    blocks_sample = blocks_prompt // 2
    r8 = tb // 8
    return pl.pallas_call(
        functools.partial(_gdn_prep_kernel, blocks_prompt=blocks_prompt, blocks_sample=blocks_sample),
        grid=(M // tb, 3),
        in_specs=[
            pl.BlockSpec((8, MIX_DIM), lambda i, j: (jnp.maximum(i * r8 - 1, 0), j)),
            pl.BlockSpec((tb, MIX_DIM), lambda i, j: (i, j)),
            pl.BlockSpec((8, MIX_DIM), lambda i, j: (jnp.minimum((i + 1) * r8, M // 8 - 1), j)),
            pl.BlockSpec((CONV_W, MIX_DIM), lambda i, j: (0, j)),
        ],
        out_specs=pl.BlockSpec((tb, MIX_DIM), lambda i, j: (i, j)),
        out_shape=jax.ShapeDtypeStruct((M, 3 * MIX_DIM), BF16),
        compiler_params=_params("parallel", "arbitrary"),
        name="gdn_prep",
    )(proj, proj, proj, conv_w.astype(F32))


def _post_mixer_kernel(*refs, gdn):
    if gdn:
        (of_ref, ob_ref, z_ref, og_ref, xq_ref, mkv_ref, wo_ref, x_ref, gf_ref, wr_ref,
         xo_ref, h_ref, aff_ref) = refs
    else:
        (mix_ref, xq_ref, mkv_ref, wo_ref, x_ref, gf_ref, wr_ref, xo_ref, h_ref, aff_ref) = refs
    scale = HEAD_DIM ** -0.5

    parts = []
    if gdn:
        for h in range(MIX_HEADS):
            lanes = slice(h * HEAD_DIM, (h + 1) * HEAD_DIM)
            o = of_ref[:, lanes] + ob_ref[:, lanes]
            o = o * lax.rsqrt(jnp.mean(o * o, axis=-1, keepdims=True) + EPS) * og_ref[...]
            z = z_ref[:, lanes]
            parts.append((o * (z * jax.nn.sigmoid(z))).astype(BF16))
    else:
        parts.append(mix_ref[...])

    for h in range(MEM_HEADS):
        q = xq_ref[:, h * HEAD_DIM:(h + 1) * HEAD_DIM].astype(BF16)
        k = mkv_ref[:, h * HEAD_DIM:(h + 1) * HEAD_DIM]
        v = mkv_ref[:, MEM_DIM + h * HEAD_DIM:MEM_DIM + (h + 1) * HEAD_DIM]
        sc = _dot_nt(q, k) * scale
        p = jnp.exp(sc - jnp.max(sc, axis=-1, keepdims=True))
        l = jnp.sum(p, axis=-1, keepdims=True)
        parts.append((_dot(p.astype(BF16), v) / l).astype(BF16))

    lhs = jnp.concatenate(parts, axis=1)
    x = x_ref[...] + _dot(lhs, wo_ref[...])
    xo_ref[...] = x
    hn = x * lax.rsqrt(jnp.mean(x * x, axis=-1, keepdims=True) + EPS) * gf_ref[...]
    h_ref[...] = hn.astype(BF16)
    logits = lax.dot_general(wr_ref[...], hn, (((1,), (1,)), ((), ())), preferred_element_type=F32,
                             precision=lax.Precision.HIGHEST)
    e = jnp.exp(logits - jnp.max(logits, axis=0, keepdims=True))
    aff_ref[...] = e / jnp.sum(e, axis=0, keepdims=True)


def post_mixer(mix_inputs, proj, xq_col_block, mkv, w_out, x, gain_ffn, w_router_t, n_tok, gdn, tm=512):
    M, D = x.shape
    tm = min(tm, n_tok // 2)
    assert n_tok % (2 * tm) == 0
    blocks_prompt = n_tok // tm
    blocks_sample = blocks_prompt // 2
    n_mem = mkv.shape[1]

    def mem_batch(i):
        return jnp.where(i < blocks_prompt, 0, 1 + (i - blocks_prompt) // blocks_sample)

    row = lambda width, cb=0: pl.BlockSpec((tm, width), lambda i: (i, cb))
    full = lambda a: pl.BlockSpec(a.shape, lambda i: (0,) * a.ndim)
    if gdn:
        o_f, o_b, out_gain = mix_inputs
        og = out_gain.reshape(1, HEAD_DIM).astype(F32)
        ins = [o_f, o_b, proj, og]
        specs = [row(MIX_DIM), row(MIX_DIM), row(MIX_DIM, 3), full(og)]
    else:
        ins = [mix_inputs[0]]
        specs = [row(MIX_DIM)]
    gf = gain_ffn.reshape(1, D).astype(F32)
    ins += [proj, mkv, w_out, x, gf, w_router_t]
    specs += [
        row(MEM_DIM, xq_col_block),
        pl.BlockSpec((None, n_mem, 2 * MEM_DIM), lambda i: (mem_batch(i), 0, 0)),
        full(w_out), row(D), full(gf), full(w_router_t),
    ]
    return pl.pallas_call(
        functools.partial(_post_mixer_kernel, gdn=gdn),
        grid=(M // tm,),
        in_specs=specs,
        out_specs=[row(D), row(D), pl.BlockSpec((N_EXPERTS, tm), lambda i: (0, i))],
        out_shape=[jax.ShapeDtypeStruct((M, D), F32), jax.ShapeDtypeStruct((M, D), BF16),
                   jax.ShapeDtypeStruct((N_EXPERTS, M), F32)],
        compiler_params=_params("parallel"),
        name="post_mixer_gdn" if gdn else "post_mixer_na",
    )(*ins)


def _expert_ffn_kernel(x_ref, gate_ref, wg_ref, wu_ref, wd_ref, o_ref, h_ref, wgb_ref, wub_ref, wdb_ref,
                       *, n_f, tf, rc):
    s = pl.program_id(1)
    cap = x_ref.shape[0]

    @pl.when(s < n_f)
    def _():
        wgb_ref[...] = wg_ref[...].astype(BF16)
        wub_ref[...] = wu_ref[...].astype(BF16)

        def body(c, carry):
            r = pl.multiple_of(c * rc, rc)
            xc = x_ref[pl.ds(r, rc), :]
            g = _dot(xc, wgb_ref[...])
            u = _dot(xc, wub_ref[...])
            h_ref[s, pl.ds(r, rc), :] = (g * jax.nn.sigmoid(g) * u).astype(BF16)
            return carry
        lax.fori_loop(0, cap // rc, body, 0)

    @pl.when(s >= n_f)
    def _():
        wdb_ref[...] = wd_ref[...].astype(BF16)

        def body(c, carry):
            r = pl.multiple_of(c * rc, rc)
            y = _dot(h_ref[0, pl.ds(r, rc), :], wdb_ref[0:tf, :])
            for f in range(1, n_f):
                y = y + _dot(h_ref[f, pl.ds(r, rc), :], wdb_ref[f * tf:(f + 1) * tf, :])
            o_ref[pl.ds(r, rc), :] = y * gate_ref[pl.ds(r, rc), :]
            return carry
        lax.fori_loop(0, cap // rc, body, 0)


def expert_ffn(xe, gate, w_gate_up, w_down, layer, tf=256, td=256):
    G, cap, D = xe.shape
    _, E, _, F2 = w_gate_up.shape
    F = F2 // 2
    tf = min(tf, F)
    td = min(td, D)
    n_f = F // tf
    n_d = D // td
    rc = min(512, cap)
    assert cap % rc == 0
    fcol = lambda s: jnp.minimum(s, n_f - 1)
    dcol = lambda s: jnp.maximum(s - n_f, 0)
    return pl.pallas_call(
        functools.partial(_expert_ffn_kernel, n_f=n_f, tf=tf, rc=rc),
        grid=(G, n_f + n_d),
        in_specs=[
            pl.BlockSpec((None, cap, D), lambda g, s: (g, 0, 0)),
            pl.BlockSpec((None, cap, 1), lambda g, s: (g, 0, 0)),
            pl.BlockSpec((None, None, D, tf), lambda g, s: (layer, g % E, 0, fcol(s))),
            pl.BlockSpec((None, None, D, tf), lambda g, s: (layer, g % E, 0, n_f + fcol(s))),
            pl.BlockSpec((None, None, F, td), lambda g, s: (layer, g % E, 0, dcol(s))),
        ],
        out_specs=pl.BlockSpec((None, cap, td), lambda g, s: (g, 0, dcol(s))),
        out_shape=jax.ShapeDtypeStruct((G, cap, D), F32),
        scratch_shapes=[pltpu.VMEM((n_f, cap, tf), BF16), pltpu.VMEM((D, tf), BF16), pltpu.VMEM((D, tf), BF16),
                        pltpu.VMEM((F, td), BF16)],
        compiler_params=_params("parallel", "arbitrary"),
        name="expert_ffn",
    )(xe, gate, w_gate_up, w_gate_up, w_down)


def _rmsnorm_kernel(x_ref, g_ref, o_ref):
    x = x_ref[...]
    o_ref[...] = x * lax.rsqrt(jnp.mean(x * x, axis=-1, keepdims=True) + EPS) * g_ref[...]


def rmsnorm_rows(x, gain, tm=512):
    M, D = x.shape
    tm = min(tm, M)
    return pl.pallas_call(
        _rmsnorm_kernel,
        grid=(M // tm,),
        in_specs=[pl.BlockSpec((tm, D), lambda i: (i, 0)), pl.BlockSpec((1, D), lambda i: (0, 0))],
        out_specs=pl.BlockSpec((tm, D), lambda i: (i, 0)),
        out_shape=jax.ShapeDtypeStruct((M, D), F32),
        compiler_params=_params("parallel"),
        name="final_rmsnorm",
    )(x, gain.reshape(1, D).astype(F32))


def _moe(x, h, aff_t, w_gate_up, w_down, layer, n_tok):
    E = aff_t.shape[0]
    cap = CAPACITY_FACTOR * n_tok // E
    gates, idxs = [], []
    for t in range(2):
        gate, idx = lax.top_k(aff_t[:, t * n_tok:(t + 1) * n_tok], cap)
        gates.append(gate)
        idxs.append(idx + t * n_tok)
    gate = jnp.concatenate(gates, axis=0)
    idx = jnp.concatenate(idxs, axis=0)
    xe = h[idx]
    ye = expert_ffn(xe, gate[..., None], w_gate_up, w_down, layer)
    return x.at[idx.reshape(-1)].add(ye.reshape(-1, x.shape[1]))


def _forward(x_prompt, x_sample, mem_prompt, mem_sample, norm_mix, na_w_in, na_rpb, gdn_w_in, gdn_conv,
             gdn_a_log, gdn_dt_bias, gdn_out_norm, norm_mem, w_mem_kv, w_out, norm_ffn, w_router,
             w_gate_up, w_down, norm_final):
    D = x_prompt.shape[-1]
    n_tok = x_prompt.shape[0] * x_prompt.shape[1]
    assert x_prompt.shape[0] == 1 and x_sample.shape[0] == 2 and x_sample.shape[1] * 2 == n_tok
    n_mem = mem_prompt.shape[1]
    depth = norm_mix.shape[0]
    x = jnp.concatenate([x_prompt.reshape(n_tok, D), x_sample.reshape(n_tok, D)], axis=0)
    mem = jnp.concatenate([mem_prompt.reshape(n_mem, D), mem_sample.reshape(2 * n_mem, D)], axis=0)

    c_qkvz = 4 * MIX_DIM
    n_gate = 2 * MIX_HEADS
    gdn_cols = c_qkvz + MEM_DIM + 2 * n_gate
    gdn_cols_padded = -(-gdn_cols // 512) * 512

    for i in range(depth):
        j = i // 2
        if i % 2 == 0:
            w_in = na_w_in[j].astype(BF16)
            proj = rms_matmul(x, norm_mix[i], w_in, BF16)
            mix = neighbourhood_attention(proj, _na_bias_table(na_rpb[j]), n_tok)
            mix_inputs = (mix,)
            xq_block = 3 * MIX_DIM // MEM_DIM
        else:
            w = gdn_w_in[j]
            w_in = jnp.concatenate(
                [w[:, :c_qkvz], w[:, c_qkvz + 2 * n_gate:], w[:, c_qkvz:c_qkvz + 2 * n_gate],
                 jnp.zeros((D, gdn_cols_padded - gdn_cols), w.dtype)], axis=1).astype(BF16)
            proj = rms_matmul(x, norm_mix[i], w_in, F32)
            qkv = gdn_prep(proj, gdn_conv[j], n_tok)
            g0 = c_qkvz + MEM_DIM
            b_t = proj[:, g0:g0 + n_gate].T
            a_t = proj[:, g0 + n_gate:g0 + 2 * n_gate].T
            o_f, o_b = gated_delta(qkv, a_t, b_t, gdn_a_log[j], gdn_dt_bias[j], n_tok)
            mix_inputs = (o_f, o_b, gdn_out_norm[j])
            xq_block = c_qkvz // MEM_DIM
        mkv = rms_matmul(mem, norm_mem[i], w_mem_kv[i].astype(BF16), BF16, tm=n_mem)
        mkv = mkv.reshape(3, n_mem, 2 * MEM_DIM)
        x, h, aff_t = post_mixer(mix_inputs, proj, xq_block, mkv, w_out[i].astype(BF16), x, norm_ffn[i],
                                 w_router[i].T.astype(F32), n_tok, gdn=(i % 2 == 1))
        x = _moe(x, h, aff_t, w_gate_up, w_down, i, n_tok)

    y = rmsnorm_rows(x, norm_final)
    return y[:n_tok].reshape(x_prompt.shape), y[n_tok:].reshape(x_sample.shape)


def kernel(x_prompt, x_sample, mem_prompt, mem_sample, norm_mix, na_w_in, na_rpb, gdn_w_in, gdn_conv, gdn_a_log, gdn_dt_bias, gdn_out_norm, norm_mem, w_mem_kv, w_out, norm_ffn, w_router, w_gate_up, w_down, norm_final):
    return _forward(x_prompt, x_sample, mem_prompt, mem_sample, norm_mix, na_w_in, na_rpb, gdn_w_in, gdn_conv,
                    gdn_a_log, gdn_dt_bias, gdn_out_norm, norm_mem, w_mem_kv, w_out, norm_ffn, w_router,
                    w_gate_up, w_down, norm_final)
```

```python
import functools
import math

import jax
import jax.numpy as jnp
import numpy as np
from jax import lax
from jax.experimental import pallas as pl
from jax.experimental.pallas import tpu as pltpu

F32 = jnp.float32
BF16 = jnp.bfloat16

HEAD_DIM = 128
MIX_HEADS = 12
MIX_DIM = MIX_HEADS * HEAD_DIM
MEM_HEADS = 4
MEM_DIM = MEM_HEADS * HEAD_DIM
GRID_W = 64
WIN_R = 8
WIN_C = 16
CONV_W = 5
CHUNK = 64
N_EXPERTS = 16
CAPACITY_FACTOR = 2
EPS = 1e-6
NEG_BIG = -1e30

V7X_VMEM_LIMIT_BYTES = 56 * 1024 * 1024
GDN_GROUP = 4 * CHUNK
GDN_HEADS_PER_STEP = 2
NA_ROWS_PER_ITER = 4
GATHER_ROWS_PER_STEP = 512
COMBINE_TILE = 256


def _params(*sem):
    return pltpu.CompilerParams(dimension_semantics=sem, vmem_limit_bytes=V7X_VMEM_LIMIT_BYTES)


def _dot(a, b):
    return jnp.dot(a, b, preferred_element_type=F32)


def _dot_nt(a, b):
    return lax.dot_general(a, b, (((1,), (1,)), ((), ())), preferred_element_type=F32)


def _dot_tn(a, b):
    return lax.dot_general(a, b, (((0,), (0,)), ((), ())), preferred_element_type=F32)


def _rms_matmul_kernel(x_ref, g_ref, w_ref, o_ref, xn_ref, *, tm, rc):
    @pl.when(pl.program_id(1) == 0)
    def _():
        def body(c, carry):
            r = pl.multiple_of(c * rc, rc)
            x = x_ref[pl.ds(r, rc), :]
            ms = jnp.mean(x * x, axis=-1, keepdims=True)
            xn_ref[pl.ds(r, rc), :] = (x * lax.rsqrt(ms + EPS) * g_ref[...]).astype(BF16)
            return carry
        lax.fori_loop(0, tm // rc, body, 0)

    o_ref[...] = _dot(xn_ref[...], w_ref[...]).astype(o_ref.dtype)


def rms_matmul(x, gain, w, out_dtype, tm=1024, tn=512):
    M, D = x.shape
    N = w.shape[1]
    tm = min(tm, M)
    tn = min(tn, N)
    assert M % tm == 0 and N % tn == 0
    rc = min(128, tm)
    return pl.pallas_call(
        functools.partial(_rms_matmul_kernel, tm=tm, rc=rc),
        grid=(M // tm, N // tn),
        in_specs=[
            pl.BlockSpec((tm, D), lambda i, j: (i, 0)),
            pl.BlockSpec((1, D), lambda i, j: (0, 0)),
            pl.BlockSpec((D, tn), lambda i, j: (0, j)),
        ],
        out_specs=pl.BlockSpec((tm, tn), lambda i, j: (i, j)),
        out_shape=jax.ShapeDtypeStruct((M, N), out_dtype),
        scratch_shapes=[pltpu.VMEM((tm, D), BF16)],
        compiler_params=_params("parallel", "arbitrary"),
        name="rms_matmul",
    )(x, gain.reshape(1, D), w)


def _na_bias_table(rpb):
    col = np.arange(GRID_W)
    cs = np.clip(col - WIN_C // 2, 0, GRID_W - WIN_C)
    c2 = np.arange(GRID_W)
    inside = (c2[None, :] >= cs[:, None]) & (c2[None, :] < cs[:, None] + WIN_C)
    dc = np.clip(c2[None, :] - col[:, None] + (WIN_C - 1), 0, 2 * WIN_C - 2)
    t = np.arange(WIN_R)
    i = np.arange(WIN_R)
    dr = t[:, None] + i[None, :]
    tab = rpb[:, dr[:, :, None, None], dc[None, None, :, :]]
    tab = jnp.where(jnp.asarray(inside)[None, None, None], tab.astype(F32), NEG_BIG)
    tab = jnp.transpose(tab, (0, 1, 3, 2, 4))
    return tab.reshape(rpb.shape[0], WIN_R, GRID_W, WIN_R * GRID_W)


def _na_kernel(q_ref, k_ref, v_ref, b_ref, o_ref, *, rows_prompt, rows_sample):
    s = pl.program_id(1)
    rows = jnp.where(s == 0, rows_prompt, rows_sample)
    base = jnp.where(s == 2, rows_sample, 0)
    scale = HEAD_DIM ** -0.5
    kw = WIN_R * GRID_W

    def one_row(r):
        rs = jnp.clip(r - WIN_R // 2, 0, rows - WIN_R)
        t = rs - r + (WIN_R - 1)
        q = q_ref[pl.ds(pl.multiple_of((base + r) * GRID_W, GRID_W), GRID_W), :]
        k0 = pl.multiple_of((base + rs) * GRID_W, GRID_W)
        k = k_ref[pl.ds(k0, kw), :]
        v = v_ref[pl.ds(k0, kw), :]
        sc = _dot_nt(q, k) * scale + b_ref[t]
        yield
        m = jnp.max(sc, axis=-1, keepdims=True)
        p = jnp.exp(sc - m)
        l = jnp.sum(p, axis=-1, keepdims=True)
        o = _dot(p.astype(BF16), v) / l
        yield
        o_ref[pl.ds(pl.multiple_of((base + r) * GRID_W, GRID_W), GRID_W), :] = o.astype(o_ref.dtype)

    def body(i, carry):
        active = [one_row(i * NA_ROWS_PER_ITER + u) for u in range(NA_ROWS_PER_ITER)]
        while active:
            active = [g for g in active if next(g, StopIteration) is not StopIteration]
        return carry

    lax.fori_loop(0, rows // NA_ROWS_PER_ITER, body, 0)


def neighbourhood_attention(proj, bias_tab, n_tok):
    M = proj.shape[0]
    assert M == 2 * n_tok
    rows_prompt = n_tok // GRID_W
    rows_sample = rows_prompt // 2
    assert rows_sample >= WIN_R and rows_sample % NA_ROWS_PER_ITER == 0
    blk = lambda off: pl.BlockSpec((n_tok, HEAD_DIM), lambda h, s: (jnp.minimum(s, 1), h + off))
    return pl.pallas_call(
        functools.partial(_na_kernel, rows_prompt=rows_prompt, rows_sample=rows_sample),
        grid=(MIX_HEADS, 3),
        in_specs=[
            blk(0), blk(MIX_HEADS), blk(2 * MIX_HEADS),
            pl.BlockSpec((None, WIN_R, GRID_W, WIN_R * GRID_W), lambda h, s: (h, 0, 0, 0)),
        ],
        out_specs=blk(0),
        out_shape=jax.ShapeDtypeStruct((M, MIX_DIM), BF16),
        compiler_params=_params("parallel", "arbitrary"),
        name="neighbourhood_attention",
    )(proj, proj, proj, bias_tab)


def _softplus(x):
    return jnp.maximum(x, 0.0) + jnp.log1p(jnp.exp(-jnp.abs(x)))


def _gdn_local(q, k, v, g_row, beta_row, reverse, store):
    G = q.shape[0]
    nchunk = G // CHUNK
    kk = _dot_nt(k, k)
    qk = _dot_nt(q, k)
    yield
    ri = lax.broadcasted_iota(jnp.int32, (G, G), 0)
    ci = lax.broadcasted_iota(jnp.int32, (G, G), 1)
    shift = int(math.log2(CHUNK))
    same = jnp.right_shift(ri, shift) == jnp.right_shift(ci, shift)
    if reverse:
        incl = same & (ci >= ri)
        strict = same & (ci > ri)
    else:
        incl = same & (ci <= ri)
        strict = same & (ci < ri)
    eye = ri == ci

    gc = jnp.sum(jnp.where(incl, g_row, 0.0), axis=1, keepdims=True)
    gl = jnp.sum(jnp.where(same, g_row, 0.0), axis=1, keepdims=True)
    beta = jnp.sum(jnp.where(eye, beta_row, 0.0), axis=1, keepdims=True)
    cm = jnp.broadcast_to(gc, (G, G))
    diff = cm - cm.T
    decay = jnp.where(incl, jnp.exp(jnp.where(incl, diff, 0.0)), 0.0)
    a_neg = jnp.where(strict, -(beta * kk * decay), 0.0)

    tm = jnp.where(eye, 1.0, 0.0) + a_neg
    pb = a_neg.astype(BF16)
    p = _dot(pb, pb)
    yield
    for it in range(int(math.log2(CHUNK)) - 1):
        pb = p.astype(BF16)
        tm = tm + _dot(tm.astype(BF16), pb)
        if it < int(math.log2(CHUNK)) - 2:
            p = _dot(pb, pb)
        yield

    egc = jnp.exp(gc)
    kf = k.astype(F32)
    rhs = jnp.concatenate([v.astype(F32) * beta, kf * (beta * egc)], axis=1)
    uw = _dot(tm.astype(BF16), rhs.astype(BF16))
    yield
    dh = q.shape[1]
    u = uw[:, :dh]
    w = uw[:, dh:].astype(BF16)
    qkd = (qk * decay).astype(BF16)
    q_dec = (q.astype(F32) * egc).astype(BF16)
    k_dec = (kf * jnp.exp(gl - gc)).astype(BF16)
    wq = jnp.concatenate(
        [t[c * CHUNK:(c + 1) * CHUNK] for c in range(nchunk) for t in (w, q_dec)], axis=0)
    qkd_blocks = jnp.concatenate(
        [qkd[c * CHUNK:(c + 1) * CHUNK, c * CHUNK:(c + 1) * CHUNK] for c in range(nchunk)], axis=0)
    store(u, wq, k_dec, qkd_blocks, jnp.broadcast_to(jnp.exp(gl), (G, dh)))


def _gdn_scan(u, wq, k_dec, qkd_blocks, egl, s_state, reverse, store):
    G = u.shape[0]
    nchunk = G // CHUNK
    outs = [None] * nchunk
    order = range(nchunk - 1, -1, -1) if reverse else range(nchunk)
    for c in order:
        lo, hi = c * CHUNK, (c + 1) * CHUNK
        r = _dot(wq[2 * lo:2 * hi], s_state.astype(BF16))
        yield
        vb = (u[lo:hi] - r[:CHUNK]).astype(BF16)
        outs[c] = r[CHUNK:] + _dot(qkd_blocks[lo:hi], vb)
        s_state = s_state * egl[lo:lo + 1, :] + _dot_tn(k_dec[lo:hi], vb)
        yield
    store(jnp.concatenate(outs, axis=0), s_state)


def _gdn_kernel(qf_ref, kf_ref, vf_ref, af_ref, bf_ref, qb_ref, kb_ref, vb_ref, ab_ref, bb_ref,
                alog_ref, dtb_ref, of_ref, ob_ref, s_ref, u_ref, wq_ref, kd_ref, qkd_ref, egl_ref,
                *, steps_prompt, steps_sample):
    hb = pl.program_id(0)
    t = pl.program_id(1)
    carried = (s_ref, u_ref, wq_ref, kd_ref, qkd_ref, egl_ref)

    @pl.when(t == 0)
    def _():
        for ref in carried:
            ref[...] = jnp.zeros_like(ref)

    prev = t - 1
    @pl.when((prev == 0) | (prev == steps_prompt) | (prev == steps_prompt + steps_sample))
    def _():
        s_ref[...] = jnp.zeros_like(s_ref)

    scans, locals_ = [], []
    for d, (q_ref, k_ref, v_ref, a_ref, b_ref, o_ref) in enumerate(
            ((qf_ref, kf_ref, vf_ref, af_ref, bf_ref, of_ref), (qb_ref, kb_ref, vb_ref, ab_ref, bb_ref, ob_ref))):
        for j in range(GDN_HEADS_PER_STEP):
            lanes = slice(j * HEAD_DIM, (j + 1) * HEAD_DIM)

            def store_scan(o, s_new, d=d, j=j, o_ref=o_ref, lanes=lanes):
                s_ref[d, j] = s_new
                o_ref[:, lanes] = o

            scans.append(_gdn_scan(u_ref[d, j], wq_ref[d, j], kd_ref[d, j], qkd_ref[d, j], egl_ref[d, j],
                                   s_ref[d, j], reverse=(d == 1), store=store_scan))

            def store_local(u, wq, k_dec, qkd_blocks, egl, d=d, j=j):
                u_ref[d, j] = u
                wq_ref[d, j] = wq
                kd_ref[d, j] = k_dec
                qkd_ref[d, j] = qkd_blocks
                egl_ref[d, j] = egl

            gate_row = d * MIX_HEADS + hb * GDN_HEADS_PER_STEP + j
            a_row = a_ref[pl.ds(gate_row, 1), :]
            b_row = b_ref[pl.ds(gate_row, 1), :]
            g_row = -jnp.exp(alog_ref[pl.ds(gate_row, 1), :]) * _softplus(a_row + dtb_ref[pl.ds(gate_row, 1), :])
            locals_.append(_gdn_local(q_ref[:, lanes], k_ref[:, lanes], v_ref[:, lanes], g_row,
                                      jax.nn.sigmoid(b_row), reverse=(d == 1), store=store_local))

    active = scans + locals_
    while active:
        active = [g for g in active if next(g, StopIteration) is not StopIteration]


def gated_delta(qkv, a_t, b_t, a_log, dt_bias, n_tok):
    M = qkv.shape[0]
    assert M == 2 * n_tok
    steps_prompt = n_tok // GDN_GROUP
    steps_sample = steps_prompt // 2
    assert steps_sample >= 1 and n_tok % (2 * GDN_GROUP) == 0
    n_steps = 2 * steps_prompt
    nh = GDN_HEADS_PER_STEP
    hw = nh * HEAD_DIM
    col_blocks = MIX_DIM // hw

    def bwd_block(t):
        in_prompt = t < steps_prompt
        in_s1 = t < steps_prompt + steps_sample
        start = jnp.where(in_prompt, 0, jnp.where(in_s1, steps_prompt, steps_prompt + steps_sample))
        length = jnp.where(in_prompt, steps_prompt, steps_sample)
        return start + length - 1 - (t - start)

    local = lambda t: jnp.minimum(t, n_steps - 1)
    scan = lambda t: jnp.maximum(t - 1, 0)
    tok_f = lambda off: pl.BlockSpec((GDN_GROUP, hw), lambda h, t: (local(t), h + off * col_blocks))
    tok_b = lambda off: pl.BlockSpec((GDN_GROUP, hw), lambda h, t: (bwd_block(local(t)), h + off * col_blocks))
    gate_f = pl.BlockSpec((2 * MIX_HEADS, GDN_GROUP), lambda h, t: (0, local(t)))
    gate_b = pl.BlockSpec((2 * MIX_HEADS, GDN_GROUP), lambda h, t: (0, bwd_block(local(t))))
    small = pl.BlockSpec((2 * MIX_HEADS, 1), lambda h, t: (0, 0))
    out_f = pl.BlockSpec((GDN_GROUP, hw), lambda h, t: (scan(t), h))
    out_b = pl.BlockSpec((GDN_GROUP, hw), lambda h, t: (bwd_block(scan(t)), h))
    out = jax.ShapeDtypeStruct((M, MIX_DIM), F32)
    per_unit = lambda shape, dtype: pltpu.VMEM((2, nh) + shape, dtype)
    return pl.pallas_call(
        functools.partial(_gdn_kernel, steps_prompt=steps_prompt, steps_sample=steps_sample),
        grid=(MIX_HEADS // nh, n_steps + 1),
        in_specs=[tok_f(0), tok_f(1), tok_f(2), gate_f, gate_f, tok_b(0), tok_b(1), tok_b(2), gate_b, gate_b,
                  small, small],
        out_specs=[out_f, out_b],
        out_shape=[out, out],
        scratch_shapes=[per_unit((HEAD_DIM, HEAD_DIM), F32), per_unit((GDN_GROUP, HEAD_DIM), F32),
                        per_unit((2 * GDN_GROUP, HEAD_DIM), BF16), per_unit((GDN_GROUP, HEAD_DIM), BF16),
                        per_unit((GDN_GROUP, CHUNK), BF16), per_unit((GDN_GROUP, HEAD_DIM), F32)],
        compiler_params=_params("parallel", "arbitrary"),
        name="gated_delta",
    )(qkv, qkv, qkv, a_t, b_t, qkv, qkv, qkv, a_t, b_t,
      a_log.reshape(2 * MIX_HEADS, 1).astype(F32), dt_bias.reshape(2 * MIX_HEADS, 1).astype(F32))


def _gdn_prep_kernel(prev_ref, x_ref, next_ref, w_ref, o_ref, *, blocks_prompt, blocks_sample):
    i = pl.program_id(0)
    part = pl.program_id(1)
    tb = x_ref.shape[0]
    s1 = blocks_prompt
    s2 = blocks_prompt + blocks_sample
    s3 = blocks_prompt + 2 * blocks_sample
    pmask = jnp.where((i == 0) | (i == s1) | (i == s2), 0.0, 1.0)
    nmask = jnp.where((i == s1 - 1) | (i == s2 - 1) | (i == s3 - 1), 0.0, 1.0)
    row = lax.broadcasted_iota(jnp.int32, (tb, HEAD_DIM), 0)
    qscale = jnp.where(part == 0, HEAD_DIM ** -0.5, 1.0)
    for h in range(MIX_HEADS):
        lanes = slice(h * HEAD_DIM, (h + 1) * HEAD_DIM)
        x = x_ref[:, lanes]
        pv = prev_ref[:, lanes] * pmask
        nx = next_ref[:, lanes] * nmask
        w = w_ref[:, lanes]
        xm1 = jnp.where(row == 0, pv[7:8], pltpu.roll(x, 1, axis=0))
        xm2 = jnp.where(row == 0, pv[6:7], jnp.where(row == 1, pv[7:8], pltpu.roll(x, 2, axis=0)))
        xp1 = jnp.where(row == tb - 1, nx[0:1], pltpu.roll(x, tb - 1, axis=0))
        xp2 = jnp.where(row == tb - 2, nx[0:1], jnp.where(row == tb - 1, nx[1:2], pltpu.roll(x, tb - 2, axis=0)))
        y = xm2 * w[0:1] + xm1 * w[1:2] + x * w[2:3] + xp1 * w[3:4] + xp2 * w[4:5]
        y = y * jax.nn.sigmoid(y)
        inv = lax.rsqrt(jnp.sum(y * y, axis=-1, keepdims=True) + EPS) * qscale
        o_ref[:, lanes] = (y * jnp.where(part == 2, 1.0, inv)).astype(o_ref.dtype)


def gdn_prep(proj, conv_w, n_tok, tb=256):
    M = proj.shape[0]
    assert CONV_W == 5 and M == 2 * n_tok
    tb = min(tb, n_tok // 2)
    assert n_tok % (2 * tb) == 0 and tb % 8 == 0
    blocks_prompt = n_tok // tb
    blocks_sample = blocks_prompt // 2
    r8 = tb // 8
    return pl.pallas_call(
        functools.partial(_gdn_prep_kernel, blocks_prompt=blocks_prompt, blocks_sample=blocks_sample),
        grid=(M // tb, 3),
        in_specs=[
            pl.BlockSpec((8, MIX_DIM), lambda i, j: (jnp.maximum(i * r8 - 1, 0), j)),
            pl.BlockSpec((tb, MIX_DIM), lambda i, j: (i, j)),
            pl.BlockSpec((8, MIX_DIM), lambda i, j: (jnp.minimum((i + 1) * r8, M // 8 - 1), j)),
            pl.BlockSpec((CONV_W, MIX_DIM), lambda i, j: (0, j)),
        ],
        out_specs=pl.BlockSpec((tb, MIX_DIM), lambda i, j: (i, j)),
        out_shape=jax.ShapeDtypeStruct((M, 3 * MIX_DIM), BF16),
        compiler_params=_params("parallel", "arbitrary"),
        name="gdn_prep",
    )(proj, proj, proj, conv_w.astype(F32))


def _post_mixer_kernel(*refs, gdn):
    if gdn:
        (of_ref, ob_ref, z_ref, og_ref, xq_ref, mkv_ref, wo_ref, x_ref, gf_ref, wr_ref,
         xo_ref, h_ref, aff_ref) = refs
    else:
        (mix_ref, xq_ref, mkv_ref, wo_ref, x_ref, gf_ref, wr_ref, xo_ref, h_ref, aff_ref) = refs
    scale = HEAD_DIM ** -0.5

    parts = []
    if gdn:
        for h in range(MIX_HEADS):
            lanes = slice(h * HEAD_DIM, (h + 1) * HEAD_DIM)
            o = of_ref[:, lanes] + ob_ref[:, lanes]
            o = o * lax.rsqrt(jnp.mean(o * o, axis=-1, keepdims=True) + EPS) * og_ref[...]
            z = z_ref[:, lanes]
            parts.append((o * (z * jax.nn.sigmoid(z))).astype(BF16))
    else:
        parts.append(mix_ref[...])

    for h in range(MEM_HEADS):
        q = xq_ref[:, h * HEAD_DIM:(h + 1) * HEAD_DIM].astype(BF16)
        k = mkv_ref[:, h * HEAD_DIM:(h + 1) * HEAD_DIM]
        v = mkv_ref[:, MEM_DIM + h * HEAD_DIM:MEM_DIM + (h + 1) * HEAD_DIM]
        sc = _dot_nt(q, k) * scale
        p = jnp.exp(sc - jnp.max(sc, axis=-1, keepdims=True))
        l = jnp.sum(p, axis=-1, keepdims=True)
        parts.append((_dot(p.astype(BF16), v) / l).astype(BF16))

    lhs = jnp.concatenate(parts, axis=1)
    x = x_ref[...] + _dot(lhs, wo_ref[...])
    xo_ref[...] = x
    hn = x * lax.rsqrt(jnp.mean(x * x, axis=-1, keepdims=True) + EPS) * gf_ref[...]
    bits = lax.bitcast_convert_type(hn.astype(BF16).astype(F32), jnp.uint32)
    half = bits.shape[1] // 2
    h_ref[...] = bits[:, :half] | (bits[:, half:] >> 16)
    logits = lax.dot_general(wr_ref[...], hn, (((1,), (1,)), ((), ())), preferred_element_type=F32,
                             precision=lax.Precision.HIGHEST)
    e = jnp.exp(logits - jnp.max(logits, axis=0, keepdims=True))
    aff_ref[...] = e / jnp.sum(e, axis=0, keepdims=True)


def post_mixer(mix_inputs, proj, xq_col_block, mkv, w_out, x, gain_ffn, w_router_t, n_tok, gdn, tm=512):
    M, D = x.shape
    tm = min(tm, n_tok // 2)
    assert n_tok % (2 * tm) == 0
    blocks_prompt = n_tok // tm
    blocks_sample = blocks_prompt // 2
    n_mem = mkv.shape[1]

    def mem_batch(i):
        return jnp.where(i < blocks_prompt, 0, 1 + (i - blocks_prompt) // blocks_sample)

    row = lambda width, cb=0: pl.BlockSpec((tm, width), lambda i: (i, cb))
    full = lambda a: pl.BlockSpec(a.shape, lambda i: (0,) * a.ndim)
    if gdn:
        o_f, o_b, out_gain = mix_inputs
        og = out_gain.reshape(1, HEAD_DIM).astype(F32)
        ins = [o_f, o_b, proj, og]
        specs = [row(MIX_DIM), row(MIX_DIM), row(MIX_DIM, 3), full(og)]
    else:
        ins = [mix_inputs[0]]
        specs = [row(MIX_DIM)]
    gf = gain_ffn.reshape(1, D).astype(F32)
    ins += [proj, mkv, w_out, x, gf, w_router_t]
    specs += [
        row(MEM_DIM, xq_col_block),
        pl.BlockSpec((None, n_mem, 2 * MEM_DIM), lambda i: (mem_batch(i), 0, 0)),
        full(w_out), row(D), full(gf), full(w_router_t),
    ]
    return pl.pallas_call(
        functools.partial(_post_mixer_kernel, gdn=gdn),
        grid=(M // tm,),
        in_specs=specs,
        out_specs=[row(D), row(D // 2), pl.BlockSpec((N_EXPERTS, tm), lambda i: (0, i))],
        out_shape=[jax.ShapeDtypeStruct((M, D), F32), jax.ShapeDtypeStruct((M, D // 2), jnp.uint32),
                   jax.ShapeDtypeStruct((N_EXPERTS, M), F32)],
        compiler_params=_params("parallel"),
        name="post_mixer_gdn" if gdn else "post_mixer_na",
    )(*ins)


def _expert_ffn_kernel(x_ref, gate_ref, wg_ref, wu_ref, wd_ref, o_ref, h_ref, wgb_ref, wub_ref, wdb_ref, xb_ref,
                       *, n_f, tf, rc):
    s = pl.program_id(1)
    cap = x_ref.shape[0]
    half = x_ref.shape[1]

    @pl.when(s == 0)
    def _():
        def body(c, carry):
            r = pl.multiple_of(c * rc, rc)
            w = x_ref[pl.ds(r, rc), :]
            hi = lax.bitcast_convert_type(w & jnp.uint32(0xFFFF0000), F32)
            lo = lax.bitcast_convert_type(w << 16, F32)
            xb_ref[pl.ds(r, rc), 0:half] = hi.astype(BF16)
            xb_ref[pl.ds(r, rc), half:2 * half] = lo.astype(BF16)
            return carry
        lax.fori_loop(0, cap // rc, body, 0)

    @pl.when(s < n_f)
    def _():
        wgb_ref[...] = wg_ref[...].astype(BF16)
        wub_ref[...] = wu_ref[...].astype(BF16)

        def body(c, carry):
            r = pl.multiple_of(c * rc, rc)
            xc = xb_ref[pl.ds(r, rc), :]
            g = _dot(xc, wgb_ref[...])
            u = _dot(xc, wub_ref[...])
            h_ref[s, pl.ds(r, rc), :] = (g * jax.nn.sigmoid(g) * u).astype(BF16)
            return carry
        lax.fori_loop(0, cap // rc, body, 0)

    @pl.when(s >= n_f)
    def _():
        wdb_ref[...] = wd_ref[...].astype(BF16)

        def body(c, carry):
            r = pl.multiple_of(c * rc, rc)
            y = _dot(h_ref[0, pl.ds(r, rc), :], wdb_ref[0:tf, :])
            for f in range(1, n_f):
                y = y + _dot(h_ref[f, pl.ds(r, rc), :], wdb_ref[f * tf:(f + 1) * tf, :])
            o_ref[pl.ds(r, rc), :] = y * gate_ref[pl.ds(r, rc), :]
            return carry
        lax.fori_loop(0, cap // rc, body, 0)


def expert_ffn(xe, gate, w_gate_up, w_down, layer, tf=256, td=256):
    G, cap, half = xe.shape
    D = 2 * half
    _, E, _, F2 = w_gate_up.shape
    F = F2 // 2
    tf = min(tf, F)
    td = min(td, D)
    n_f = F // tf
    n_d = D // td
    rc = min(512, cap)
    assert cap % rc == 0
    fcol = lambda s: jnp.minimum(s, n_f - 1)
    dcol = lambda s: jnp.maximum(s - n_f, 0)
    return pl.pallas_call(
        functools.partial(_expert_ffn_kernel, n_f=n_f, tf=tf, rc=rc),
        grid=(G, n_f + n_d),
        in_specs=[
            pl.BlockSpec((None, cap, half), lambda g, s: (g, 0, 0)),
            pl.BlockSpec((None, cap, 1), lambda g, s: (g, 0, 0)),
            pl.BlockSpec((None, None, D, tf), lambda g, s: (layer, g % E, 0, fcol(s))),
            pl.BlockSpec((None, None, D, tf), lambda g, s: (layer, g % E, 0, n_f + fcol(s))),
            pl.BlockSpec((None, None, F, td), lambda g, s: (layer, g % E, 0, dcol(s))),
        ],
        out_specs=pl.BlockSpec((None, cap, td), lambda g, s: (g, 0, dcol(s))),
        out_shape=jax.ShapeDtypeStruct((G, cap, D), F32),
        scratch_shapes=[pltpu.VMEM((n_f, cap, tf), BF16), pltpu.VMEM((D, tf), BF16), pltpu.VMEM((D, tf), BF16),
                        pltpu.VMEM((F, td), BF16), pltpu.VMEM((cap, D), BF16)],
        compiler_params=_params("parallel", "arbitrary"),
        name="expert_ffn",
    )(xe, gate, w_gate_up, w_gate_up, w_down)


def _row_gather_kernel(idx_ref, src_ref, dst_ref, sems, *, rows):
    i = pl.program_id(0)
    slot = i % 2
    base = i * rows

    def issue(r, carry):
        pltpu.make_async_copy(src_ref.at[pl.ds(idx_ref[0, r], 1)], dst_ref.at[pl.ds(base + r, 1)],
                              sems.at[slot]).start()
        return carry
    lax.fori_loop(0, rows, issue, 0, unroll=8)

    def wait_step(which):
        pltpu.make_async_copy(src_ref.at[pl.ds(0, rows)], dst_ref.at[pl.ds(0, rows)], sems.at[which]).wait()

    @pl.when(i > 0)
    def _():
        wait_step(1 - slot)

    @pl.when(i == pl.num_programs(0) - 1)
    def _():
        wait_step(slot)


def row_gather(src, idx, rows=GATHER_ROWS_PER_STEP):
    n = idx.shape[0]
    rows = min(rows, n)
    assert n % rows == 0 and src.dtype.itemsize == 4
    steps = n // rows
    return pl.pallas_call(
        functools.partial(_row_gather_kernel, rows=rows),
        grid=(steps,),
        in_specs=[pl.BlockSpec((None, 1, rows), lambda i: (i, 0, 0), memory_space=pltpu.SMEM),
                  pl.BlockSpec(memory_space=pl.ANY)],
        out_specs=pl.BlockSpec(memory_space=pl.ANY),
        out_shape=jax.ShapeDtypeStruct((n, src.shape[1]), src.dtype),
        scratch_shapes=[pltpu.SemaphoreType.DMA((2,))],
        compiler_params=_params("arbitrary"),
        name="row_gather",
    )(idx.astype(jnp.int32).reshape(steps, 1, rows), src)


def _combine_kernel(tile_ref, blk_ref, flag_ref, tok_ref, ys_ref, x_ref, g_ref, o_ref, acc_ref, *, final):
    s = pl.program_id(0)
    flag = flag_ref[s]
    T = acc_ref.shape[0]
    B = ys_ref.shape[0]

    @pl.when((flag & 1) != 0)
    def _():
        acc_ref[...] = jnp.zeros_like(acc_ref)

    @pl.when((flag & 4) != 0)
    def _():
        tokens = lax.broadcasted_iota(jnp.int32, (T, B), 0) + tile_ref[s] * T
        onehot = jnp.where(tok_ref[...] == tokens, 1.0, 0.0).astype(BF16)
        ys = ys_ref[...]
        hi = ys.astype(BF16)
        lo = (ys - hi.astype(F32)).astype(BF16)
        acc_ref[...] += _dot(onehot, hi) + _dot(onehot, lo)

    @pl.when((flag & 2) != 0)
    def _():
        y = x_ref[...] + acc_ref[...]
        if final:
            y = y * lax.rsqrt(jnp.mean(y * y, axis=-1, keepdims=True) + EPS) * g_ref[...]
        o_ref[...] = y


def _combine_schedule(tok_sorted, n_tiles, T, B):
    n_blocks = tok_sorted.shape[0] // B
    bounds = jnp.searchsorted(tok_sorted, jnp.arange(n_tiles + 1, dtype=jnp.int32) * T, side='left').astype(jnp.int32)
    lo, hi = bounds[:-1], bounds[1:]
    first_b = jnp.minimum(lo // B, n_blocks - 1)
    last_b = jnp.where(hi > lo, (hi - 1) // B, first_b)
    n = last_b - first_b + 1
    start = jnp.cumsum(n) - n
    total = jnp.sum(n)
    s = jnp.arange(n_blocks + n_tiles, dtype=jnp.int32)
    tile = jnp.clip(jnp.searchsorted(start, s, side='right').astype(jnp.int32) - 1, 0, n_tiles - 1)
    valid = s < total
    off = s - start[tile]
    blk = jnp.where(valid, first_b[tile] + off, n_blocks - 1)
    flag = jnp.where(valid, 4 + (off == 0) + 2 * (off == n[tile] - 1), 0)
    tile = jnp.where(valid, tile, n_tiles - 1)
    return tile.astype(jnp.int32), blk.astype(jnp.int32), flag.astype(jnp.int32)


def combine(x, ys, tok_sorted, final_gain=None, T=COMBINE_TILE, B=COMBINE_TILE):
    M, D = x.shape
    L = ys.shape[0]
    T = min(T, M)
    B = min(B, L)
    assert M % T == 0 and L % B == 0
    n_tiles, n_blocks = M // T, L // B
    tile, blk, flag = _combine_schedule(tok_sorted, n_tiles, T, B)
    final = final_gain is not None
    gain = (final_gain if final else jnp.ones((D,), F32)).reshape(1, D).astype(F32)
    grid_spec = pltpu.PrefetchScalarGridSpec(
        num_scalar_prefetch=3,
        grid=(n_blocks + n_tiles,),
        in_specs=[
            pl.BlockSpec((None, 1, B), lambda s, tile, blk, flag: (blk[s], 0, 0)),
            pl.BlockSpec((B, D), lambda s, tile, blk, flag: (blk[s], 0)),
            pl.BlockSpec((T, D), lambda s, tile, blk, flag: (tile[s], 0)),
            pl.BlockSpec((1, D), lambda s, tile, blk, flag: (0, 0)),
        ],
        out_specs=pl.BlockSpec((T, D), lambda s, tile, blk, flag: (tile[s], 0)),
        scratch_shapes=[pltpu.VMEM((T, D), F32)],
    )
    return pl.pallas_call(
        functools.partial(_combine_kernel, final=final),
        grid_spec=grid_spec,
        out_shape=jax.ShapeDtypeStruct((M, D), F32),
        compiler_params=_params("arbitrary"),
        name="moe_combine",
    )(tile, blk, flag, tok_sorted.astype(jnp.int32).reshape(n_blocks, 1, B), ys, x, gain)


def _moe(x, h_packed, aff_t, w_gate_up, w_down, layer, n_tok, final_gain=None):
    E = aff_t.shape[0]
    D = x.shape[1]
    cap = CAPACITY_FACTOR * n_tok // E
    gates, idxs = [], []
    for t in range(2):
        gate, idx = lax.top_k(aff_t[:, t * n_tok:(t + 1) * n_tok], cap)
        gates.append(gate)
        idxs.append(idx + t * n_tok)
    gate = jnp.concatenate(gates, axis=0)
    tok = jnp.concatenate(idxs, axis=0).reshape(-1)
    xe = row_gather(h_packed, tok).reshape(2 * E, cap, D // 2)
    ye = expert_ffn(xe, gate[..., None], w_gate_up, w_down, layer)
    order = jnp.argsort(tok).astype(jnp.int32)
    ys = row_gather(ye.reshape(-1, D), order)
    return combine(x, ys, tok[order], final_gain)


def _forward(x_prompt, x_sample, mem_prompt, mem_sample, norm_mix, na_w_in, na_rpb, gdn_w_in, gdn_conv,
             gdn_a_log, gdn_dt_bias, gdn_out_norm, norm_mem, w_mem_kv, w_out, norm_ffn, w_router,
             w_gate_up, w_down, norm_final):
    D = x_prompt.shape[-1]
    n_tok = x_prompt.shape[0] * x_prompt.shape[1]
    assert x_prompt.shape[0] == 1 and x_sample.shape[0] == 2 and x_sample.shape[1] * 2 == n_tok
    n_mem = mem_prompt.shape[1]
    depth = norm_mix.shape[0]
    x = jnp.concatenate([x_prompt.reshape(n_tok, D), x_sample.reshape(n_tok, D)], axis=0)
    mem = jnp.concatenate([mem_prompt.reshape(n_mem, D), mem_sample.reshape(2 * n_mem, D)], axis=0)

    c_qkvz = 4 * MIX_DIM
    n_gate = 2 * MIX_HEADS
    gdn_cols = c_qkvz + MEM_DIM + 2 * n_gate
    gdn_cols_padded = -(-gdn_cols // 512) * 512

    for i in range(depth):
        j = i // 2
        if i % 2 == 0:
            w_in = na_w_in[j].astype(BF16)
            proj = rms_matmul(x, norm_mix[i], w_in, BF16)
            mix = neighbourhood_attention(proj, _na_bias_table(na_rpb[j]), n_tok)
            mix_inputs = (mix,)
            xq_block = 3 * MIX_DIM // MEM_DIM
        else:
            w = gdn_w_in[j]
            w_in = jnp.concatenate(
                [w[:, :c_qkvz], w[:, c_qkvz + 2 * n_gate:], w[:, c_qkvz:c_qkvz + 2 * n_gate],
                 jnp.zeros((D, gdn_cols_padded - gdn_cols), w.dtype)], axis=1).astype(BF16)
            proj = rms_matmul(x, norm_mix[i], w_in, F32)
            qkv = gdn_prep(proj, gdn_conv[j], n_tok)
            g0 = c_qkvz + MEM_DIM
            b_t = proj[:, g0:g0 + n_gate].T
            a_t = proj[:, g0 + n_gate:g0 + 2 * n_gate].T
            o_f, o_b = gated_delta(qkv, a_t, b_t, gdn_a_log[j], gdn_dt_bias[j], n_tok)
            mix_inputs = (o_f, o_b, gdn_out_norm[j])
            xq_block = c_qkvz // MEM_DIM
        mkv = rms_matmul(mem, norm_mem[i], w_mem_kv[i].astype(BF16), BF16, tm=n_mem)
        mkv = mkv.reshape(3, n_mem, 2 * MEM_DIM)
        x, h, aff_t = post_mixer(mix_inputs, proj, xq_block, mkv, w_out[i].astype(BF16), x, norm_ffn[i],
                                 w_router[i].T.astype(F32), n_tok, gdn=(i % 2 == 1))
        x = _moe(x, h, aff_t, w_gate_up, w_down, i, n_tok, final_gain=norm_final if i == depth - 1 else None)

    y = x
    return y[:n_tok].reshape(x_prompt.shape), y[n_tok:].reshape(x_sample.shape)


def kernel(x_prompt, x_sample, mem_prompt, mem_sample, norm_mix, na_w_in, na_rpb, gdn_w_in, gdn_conv, gdn_a_log, gdn_dt_bias, gdn_out_norm, norm_mem, w_mem_kv, w_out, norm_ffn, w_router, w_gate_up, w_down, norm_final):
    return _forward(x_prompt, x_sample, mem_prompt, mem_sample, norm_mix, na_w_in, na_rpb, gdn_w_in, gdn_conv,
                    gdn_a_log, gdn_dt_bias, gdn_out_norm, norm_mem, w_mem_kv, w_out, norm_ffn, w_router,
                    w_gate_up, w_down, norm_final)
```

```python
import functools
import math

import jax
import jax.numpy as jnp
import numpy as np
from jax import lax
from jax.experimental import pallas as pl
from jax.experimental.pallas import tpu as pltpu

F32 = jnp.float32
BF16 = jnp.bfloat16

HEAD_DIM = 128
MIX_HEADS = 12
MIX_DIM = MIX_HEADS * HEAD_DIM
MEM_HEADS = 4
MEM_DIM = MEM_HEADS * HEAD_DIM
GRID_W = 64
WIN_R = 8
WIN_C = 16
CONV_W = 5
CHUNK = 64
N_EXPERTS = 16
CAPACITY_FACTOR = 2
EPS = 1e-6
NEG_BIG = -1e30

V7X_VMEM_LIMIT_BYTES = 56 * 1024 * 1024
GDN_GROUP = 4 * CHUNK
GDN_HEADS_PER_STEP = 2
NA_ROWS_PER_ITER = 4
COMBINE_TILE = 256


def _params(*sem):
    return pltpu.CompilerParams(dimension_semantics=sem, vmem_limit_bytes=V7X_VMEM_LIMIT_BYTES)


def _dot(a, b):
    return jnp.dot(a, b, preferred_element_type=F32)


def _dot_nt(a, b):
    return lax.dot_general(a, b, (((1,), (1,)), ((), ())), preferred_element_type=F32)


def _dot_tn(a, b):
    return lax.dot_general(a, b, (((0,), (0,)), ((), ())), preferred_element_type=F32)


def _rms_matmul_kernel(x_ref, g_ref, w_ref, o_ref, xn_ref, *, tm, rc):
    @pl.when(pl.program_id(1) == 0)
    def _():
        def body(c, carry):
            r = pl.multiple_of(c * rc, rc)
            x = x_ref[pl.ds(r, rc), :]
            ms = jnp.mean(x * x, axis=-1, keepdims=True)
            xn_ref[pl.ds(r, rc), :] = (x * lax.rsqrt(ms + EPS) * g_ref[...]).astype(BF16)
            return carry
        lax.fori_loop(0, tm // rc, body, 0)

    o_ref[...] = _dot(xn_ref[...], w_ref[...]).astype(o_ref.dtype)


def rms_matmul(x, gain, w, out_dtype, tm=1024, tn=512):
    M, D = x.shape
    N = w.shape[1]
    tm = min(tm, M)
    tn = min(tn, N)
    assert M % tm == 0 and N % tn == 0
    rc = min(128, tm)
    return pl.pallas_call(
        functools.partial(_rms_matmul_kernel, tm=tm, rc=rc),
        grid=(M // tm, N // tn),
        in_specs=[
            pl.BlockSpec((tm, D), lambda i, j: (i, 0)),
            pl.BlockSpec((1, D), lambda i, j: (0, 0)),
            pl.BlockSpec((D, tn), lambda i, j: (0, j)),
        ],
        out_specs=pl.BlockSpec((tm, tn), lambda i, j: (i, j)),
        out_shape=jax.ShapeDtypeStruct((M, N), out_dtype),
        scratch_shapes=[pltpu.VMEM((tm, D), BF16)],
        compiler_params=_params("parallel", "arbitrary"),
        name="rms_matmul",
    )(x, gain.reshape(1, D), w)


def _na_bias_table(rpb):
    col = np.arange(GRID_W)
    cs = np.clip(col - WIN_C // 2, 0, GRID_W - WIN_C)
    c2 = np.arange(GRID_W)
    inside = (c2[None, :] >= cs[:, None]) & (c2[None, :] < cs[:, None] + WIN_C)
    pad = GRID_W - WIN_C
    padded = jnp.pad(rpb.astype(F32), ((0, 0), (0, 0), (pad, pad)))
    by_col = jnp.stack([padded[:, :, GRID_W - 1 - c:2 * GRID_W - 1 - c] for c in range(GRID_W)], axis=2)
    tab = jnp.stack([by_col[:, t:t + WIN_R] for t in range(WIN_R)], axis=1)
    tab = jnp.where(jnp.asarray(inside)[None, None, None], tab, NEG_BIG)
    tab = jnp.transpose(tab, (0, 1, 3, 2, 4))
    return tab.reshape(rpb.shape[0], WIN_R, GRID_W, WIN_R * GRID_W)


def _na_kernel(q_ref, k_ref, v_ref, b_ref, o_ref, *, rows_prompt, rows_sample):
    s = pl.program_id(1)
    rows = jnp.where(s == 0, rows_prompt, rows_sample)
    base = jnp.where(s == 2, rows_sample, 0)
    scale = HEAD_DIM ** -0.5
    kw = WIN_R * GRID_W

    def one_row(r):
        rs = jnp.clip(r - WIN_R // 2, 0, rows - WIN_R)
        t = rs - r + (WIN_R - 1)
        q = q_ref[pl.ds(pl.multiple_of((base + r) * GRID_W, GRID_W), GRID_W), :]
        k0 = pl.multiple_of((base + rs) * GRID_W, GRID_W)
        k = k_ref[pl.ds(k0, kw), :]
        v = v_ref[pl.ds(k0, kw), :]
        sc = _dot_nt(q, k) * scale + b_ref[t]
        yield
        m = jnp.max(sc, axis=-1, keepdims=True)
        p = jnp.exp(sc - m)
        l = jnp.sum(p, axis=-1, keepdims=True)
        o = _dot(p.astype(BF16), v) / l
        yield
        o_ref[pl.ds(pl.multiple_of((base + r) * GRID_W, GRID_W), GRID_W), :] = o.astype(o_ref.dtype)

    def body(i, carry):
        active = [one_row(i * NA_ROWS_PER_ITER + u) for u in range(NA_ROWS_PER_ITER)]
        while active:
            active = [g for g in active if next(g, StopIteration) is not StopIteration]
        return carry

    lax.fori_loop(0, rows // NA_ROWS_PER_ITER, body, 0)


def neighbourhood_attention(proj, bias_tab, n_tok):
    M = proj.shape[0]
    assert M == 2 * n_tok
    rows_prompt = n_tok // GRID_W
    rows_sample = rows_prompt // 2
    assert rows_sample >= WIN_R and rows_sample % NA_ROWS_PER_ITER == 0
    blk = lambda off: pl.BlockSpec((n_tok, HEAD_DIM), lambda h, s: (jnp.minimum(s, 1), h + off))
    return pl.pallas_call(
        functools.partial(_na_kernel, rows_prompt=rows_prompt, rows_sample=rows_sample),
        grid=(MIX_HEADS, 3),
        in_specs=[
            blk(0), blk(MIX_HEADS), blk(2 * MIX_HEADS),
            pl.BlockSpec((None, WIN_R, GRID_W, WIN_R * GRID_W), lambda h, s: (h, 0, 0, 0)),
        ],
        out_specs=blk(0),
        out_shape=jax.ShapeDtypeStruct((M, MIX_DIM), BF16),
        compiler_params=_params("parallel", "arbitrary"),
        name="neighbourhood_attention",
    )(proj, proj, proj, bias_tab)


def _softplus(x):
    return jnp.maximum(x, 0.0) + jnp.log1p(jnp.exp(-jnp.abs(x)))


def _gdn_local(q, k, v, g_row, beta_row, reverse, store):
    G = q.shape[0]
    nchunk = G // CHUNK
    kk = _dot_nt(k, k)
    qk = _dot_nt(q, k)
    yield
    ri = lax.broadcasted_iota(jnp.int32, (G, G), 0)
    ci = lax.broadcasted_iota(jnp.int32, (G, G), 1)
    shift = int(math.log2(CHUNK))
    same = jnp.right_shift(ri, shift) == jnp.right_shift(ci, shift)
    if reverse:
        incl = same & (ci >= ri)
        strict = same & (ci > ri)
    else:
        incl = same & (ci <= ri)
        strict = same & (ci < ri)
    eye = ri == ci

    gc = jnp.sum(jnp.where(incl, g_row, 0.0), axis=1, keepdims=True)
    gl = jnp.sum(jnp.where(same, g_row, 0.0), axis=1, keepdims=True)
    beta = jnp.sum(jnp.where(eye, beta_row, 0.0), axis=1, keepdims=True)
    cm = jnp.broadcast_to(gc, (G, G))
    diff = cm - cm.T
    decay = jnp.where(incl, jnp.exp(jnp.where(incl, diff, 0.0)), 0.0)
    a_neg = jnp.where(strict, -(beta * kk * decay), 0.0)

    tm = jnp.where(eye, 1.0, 0.0) + a_neg
    pb = a_neg.astype(BF16)
    p = _dot(pb, pb)
    yield
    for it in range(int(math.log2(CHUNK)) - 1):
        pb = p.astype(BF16)
        tm = tm + _dot(tm.astype(BF16), pb)
        if it < int(math.log2(CHUNK)) - 2:
            p = _dot(pb, pb)
        yield

    egc = jnp.exp(gc)
    kf = k.astype(F32)
    rhs = jnp.concatenate([v.astype(F32) * beta, kf * (beta * egc)], axis=1)
    uw = _dot(tm.astype(BF16), rhs.astype(BF16))
    yield
    dh = q.shape[1]
    u = uw[:, :dh]
    w = uw[:, dh:].astype(BF16)
    qkd = (qk * decay).astype(BF16)
    q_dec = (q.astype(F32) * egc).astype(BF16)
    k_dec = (kf * jnp.exp(gl - gc)).astype(BF16)
    wq = jnp.concatenate(
        [t[c * CHUNK:(c + 1) * CHUNK] for c in range(nchunk) for t in (w, q_dec)], axis=0)
    qkd_blocks = jnp.concatenate(
        [qkd[c * CHUNK:(c + 1) * CHUNK, c * CHUNK:(c + 1) * CHUNK] for c in range(nchunk)], axis=0)
    store(u, wq, k_dec, qkd_blocks, jnp.broadcast_to(jnp.exp(gl), (G, dh)))


def _gdn_scan(u, wq, k_dec, qkd_blocks, egl, s_state, reverse, store):
    G = u.shape[0]
    nchunk = G // CHUNK
    outs = [None] * nchunk
    order = range(nchunk - 1, -1, -1) if reverse else range(nchunk)
    for c in order:
        lo, hi = c * CHUNK, (c + 1) * CHUNK
        r = _dot(wq[2 * lo:2 * hi], s_state.astype(BF16))
        yield
        vb = (u[lo:hi] - r[:CHUNK]).astype(BF16)
        outs[c] = r[CHUNK:] + _dot(qkd_blocks[lo:hi], vb)
        s_state = s_state * egl[lo:lo + 1, :] + _dot_tn(k_dec[lo:hi], vb)
        yield
    store(jnp.concatenate(outs, axis=0), s_state)


def _gdn_kernel(qf_ref, kf_ref, vf_ref, af_ref, bf_ref, qb_ref, kb_ref, vb_ref, ab_ref, bb_ref,
                alog_ref, dtb_ref, of_ref, ob_ref, s_ref, u_ref, wq_ref, kd_ref, qkd_ref, egl_ref,
                *, steps_prompt, steps_sample):
    hb = pl.program_id(0)
    t = pl.program_id(1)
    carried = (s_ref, u_ref, wq_ref, kd_ref, qkd_ref, egl_ref)

    @pl.when(t == 0)
    def _():
        for ref in carried:
            ref[...] = jnp.zeros_like(ref)

    prev = t - 1
    @pl.when((prev == 0) | (prev == steps_prompt) | (prev == steps_prompt + steps_sample))
    def _():
        s_ref[...] = jnp.zeros_like(s_ref)

    scans, locals_ = [], []
    for d, (q_ref, k_ref, v_ref, a_ref, b_ref, o_ref) in enumerate(
            ((qf_ref, kf_ref, vf_ref, af_ref, bf_ref, of_ref), (qb_ref, kb_ref, vb_ref, ab_ref, bb_ref, ob_ref))):
        for j in range(GDN_HEADS_PER_STEP):
            lanes = slice(j * HEAD_DIM, (j + 1) * HEAD_DIM)

            def store_scan(o, s_new, d=d, j=j, o_ref=o_ref, lanes=lanes):
                s_ref[d, j] = s_new
                o_ref[:, lanes] = o

            scans.append(_gdn_scan(u_ref[d, j], wq_ref[d, j], kd_ref[d, j], qkd_ref[d, j], egl_ref[d, j],
                                   s_ref[d, j], reverse=(d == 1), store=store_scan))

            def store_local(u, wq, k_dec, qkd_blocks, egl, d=d, j=j):
                u_ref[d, j] = u
                wq_ref[d, j] = wq
                kd_ref[d, j] = k_dec
                qkd_ref[d, j] = qkd_blocks
                egl_ref[d, j] = egl

            gate_row = d * MIX_HEADS + hb * GDN_HEADS_PER_STEP + j
            a_row = a_ref[pl.ds(gate_row, 1), :]
            b_row = b_ref[pl.ds(gate_row, 1), :]
            g_row = -jnp.exp(alog_ref[pl.ds(gate_row, 1), :]) * _softplus(a_row + dtb_ref[pl.ds(gate_row, 1), :])
            locals_.append(_gdn_local(q_ref[:, lanes], k_ref[:, lanes], v_ref[:, lanes], g_row,
                                      jax.nn.sigmoid(b_row), reverse=(d == 1), store=store_local))

    active = scans + locals_
    while active:
        active = [g for g in active if next(g, StopIteration) is not StopIteration]


def gated_delta(qkv, a_t, b_t, a_log, dt_bias, n_tok):
    M = qkv.shape[0]
    assert M == 2 * n_tok
    steps_prompt = n_tok // GDN_GROUP
    steps_sample = steps_prompt // 2
    assert steps_sample >= 1 and n_tok % (2 * GDN_GROUP) == 0
    n_steps = 2 * steps_prompt
    nh = GDN_HEADS_PER_STEP
    hw = nh * HEAD_DIM
    col_blocks = MIX_DIM // hw

    def bwd_block(t):
        in_prompt = t < steps_prompt
        in_s1 = t < steps_prompt + steps_sample
        start = jnp.where(in_prompt, 0, jnp.where(in_s1, steps_prompt, steps_prompt + steps_sample))
        length = jnp.where(in_prompt, steps_prompt, steps_sample)
        return start + length - 1 - (t - start)

    local = lambda t: jnp.minimum(t, n_steps - 1)
    scan = lambda t: jnp.maximum(t - 1, 0)
    tok_f = lambda off: pl.BlockSpec((GDN_GROUP, hw), lambda h, t: (local(t), h + off * col_blocks))
    tok_b = lambda off: pl.BlockSpec((GDN_GROUP, hw), lambda h, t: (bwd_block(local(t)), h + off * col_blocks))
    gate_f = pl.BlockSpec((2 * MIX_HEADS, GDN_GROUP), lambda h, t: (0, local(t)))
    gate_b = pl.BlockSpec((2 * MIX_HEADS, GDN_GROUP), lambda h, t: (0, bwd_block(local(t))))
    small = pl.BlockSpec((2 * MIX_HEADS, 1), lambda h, t: (0, 0))
    out_f = pl.BlockSpec((GDN_GROUP, hw), lambda h, t: (scan(t), h))
    out_b = pl.BlockSpec((GDN_GROUP, hw), lambda h, t: (bwd_block(scan(t)), h))
    out = jax.ShapeDtypeStruct((M, MIX_DIM), F32)
    per_unit = lambda shape, dtype: pltpu.VMEM((2, nh) + shape, dtype)
    return pl.pallas_call(
        functools.partial(_gdn_kernel, steps_prompt=steps_prompt, steps_sample=steps_sample),
        grid=(MIX_HEADS // nh, n_steps + 1),
        in_specs=[tok_f(0), tok_f(1), tok_f(2), gate_f, gate_f, tok_b(0), tok_b(1), tok_b(2), gate_b, gate_b,
                  small, small],
        out_specs=[out_f, out_b],
        out_shape=[out, out],
        scratch_shapes=[per_unit((HEAD_DIM, HEAD_DIM), F32), per_unit((GDN_GROUP, HEAD_DIM), F32),
                        per_unit((2 * GDN_GROUP, HEAD_DIM), BF16), per_unit((GDN_GROUP, HEAD_DIM), BF16),
                        per_unit((GDN_GROUP, CHUNK), BF16), per_unit((GDN_GROUP, HEAD_DIM), F32)],
        compiler_params=_params("parallel", "arbitrary"),
        name="gated_delta",
    )(qkv, qkv, qkv, a_t, b_t, qkv, qkv, qkv, a_t, b_t,
      a_log.reshape(2 * MIX_HEADS, 1).astype(F32), dt_bias.reshape(2 * MIX_HEADS, 1).astype(F32))


def _gdn_prep_kernel(prev_ref, x_ref, next_ref, w_ref, o_ref, *, blocks_prompt, blocks_sample):
    i = pl.program_id(0)
    part = pl.program_id(1)
    tb = x_ref.shape[0]
    s1 = blocks_prompt
    s2 = blocks_prompt + blocks_sample
    s3 = blocks_prompt + 2 * blocks_sample
    pmask = jnp.where((i == 0) | (i == s1) | (i == s2), 0.0, 1.0)
    nmask = jnp.where((i == s1 - 1) | (i == s2 - 1) | (i == s3 - 1), 0.0, 1.0)
    row = lax.broadcasted_iota(jnp.int32, (tb, HEAD_DIM), 0)
    qscale = jnp.where(part == 0, HEAD_DIM ** -0.5, 1.0)
    for h in range(MIX_HEADS):
        lanes = slice(h * HEAD_DIM, (h + 1) * HEAD_DIM)
        x = x_ref[:, lanes]
        pv = prev_ref[:, lanes] * pmask
        nx = next_ref[:, lanes] * nmask
        w = w_ref[:, lanes]
        xm1 = jnp.where(row == 0, pv[7:8], pltpu.roll(x, 1, axis=0))
        xm2 = jnp.where(row == 0, pv[6:7], jnp.where(row == 1, pv[7:8], pltpu.roll(x, 2, axis=0)))
        xp1 = jnp.where(row == tb - 1, nx[0:1], pltpu.roll(x, tb - 1, axis=0))
        xp2 = jnp.where(row == tb - 2, nx[0:1], jnp.where(row == tb - 1, nx[1:2], pltpu.roll(x, tb - 2, axis=0)))
        y = xm2 * w[0:1] + xm1 * w[1:2] + x * w[2:3] + xp1 * w[3:4] + xp2 * w[4:5]
        y = y * jax.nn.sigmoid(y)
        inv = lax.rsqrt(jnp.sum(y * y, axis=-1, keepdims=True) + EPS) * qscale
        o_ref[:, lanes] = (y * jnp.where(part == 2, 1.0, inv)).astype(o_ref.dtype)


def gdn_prep(proj, conv_w, n_tok, tb=256):
    M = proj.shape[0]
    assert CONV_W == 5 and M == 2 * n_tok
    tb = min(tb, n_tok // 2)
    assert n_tok % (2 * tb) == 0 and tb % 8 == 0
    blocks_prompt = n_tok // tb
    blocks_sample = blocks_prompt // 2
    r8 = tb // 8
    return pl.pallas_call(
        functools.partial(_gdn_prep_kernel, blocks_prompt=blocks_prompt, blocks_sample=blocks_sample),
        grid=(M // tb, 3),
        in_specs=[
            pl.BlockSpec((8, MIX_DIM), lambda i, j: (jnp.maximum(i * r8 - 1, 0), j)),
            pl.BlockSpec((tb, MIX_DIM), lambda i, j: (i, j)),
            pl.BlockSpec((8, MIX_DIM), lambda i, j: (jnp.minimum((i + 1) * r8, M // 8 - 1), j)),
            pl.BlockSpec((CONV_W, MIX_DIM), lambda i, j: (0, j)),
        ],
        out_specs=pl.BlockSpec((tb, MIX_DIM), lambda i, j: (i, j)),
        out_shape=jax.ShapeDtypeStruct((M, 3 * MIX_DIM), BF16),
        compiler_params=_params("parallel", "arbitrary"),
        name="gdn_prep",
    )(proj, proj, proj, conv_w.astype(F32))


def _post_mixer_kernel(*refs, gdn):
    if gdn:
        (of_ref, ob_ref, z_ref, og_ref, xq_ref, mkv_ref, wo_ref, x_ref, gf_ref, wr_ref,
         xo_ref, h_ref, aff_ref) = refs
    else:
        (mix_ref, xq_ref, mkv_ref, wo_ref, x_ref, gf_ref, wr_ref, xo_ref, h_ref, aff_ref) = refs
    scale = HEAD_DIM ** -0.5

    parts = []
    if gdn:
        for h in range(MIX_HEADS):
            lanes = slice(h * HEAD_DIM, (h + 1) * HEAD_DIM)
            o = of_ref[:, lanes] + ob_ref[:, lanes]
            o = o * lax.rsqrt(jnp.mean(o * o, axis=-1, keepdims=True) + EPS) * og_ref[...]
            z = z_ref[:, lanes]
            parts.append((o * (z * jax.nn.sigmoid(z))).astype(BF16))
    else:
        parts.append(mix_ref[...])

    for h in range(MEM_HEADS):
        q = xq_ref[:, h * HEAD_DIM:(h + 1) * HEAD_DIM].astype(BF16)
        k = mkv_ref[:, h * HEAD_DIM:(h + 1) * HEAD_DIM]
        v = mkv_ref[:, MEM_DIM + h * HEAD_DIM:MEM_DIM + (h + 1) * HEAD_DIM]
        sc = _dot_nt(q, k) * scale
        p = jnp.exp(sc - jnp.max(sc, axis=-1, keepdims=True))
        l = jnp.sum(p, axis=-1, keepdims=True)
        parts.append((_dot(p.astype(BF16), v) / l).astype(BF16))

    lhs = jnp.concatenate(parts, axis=1)
    x = x_ref[...] + _dot(lhs, wo_ref[...])
    xo_ref[...] = x
    hn = x * lax.rsqrt(jnp.mean(x * x, axis=-1, keepdims=True) + EPS) * gf_ref[...]
    bits = lax.bitcast_convert_type(hn.astype(BF16).astype(F32), jnp.uint32)
    half = bits.shape[1] // 2
    h_ref[...] = bits[:, :half] | (bits[:, half:] >> 16)
    logits = lax.dot_general(wr_ref[...], hn, (((1,), (1,)), ((), ())), preferred_element_type=F32,
                             precision=lax.Precision.HIGHEST)
    e = jnp.exp(logits - jnp.max(logits, axis=0, keepdims=True))
    aff_ref[...] = e / jnp.sum(e, axis=0, keepdims=True)


def post_mixer(mix_inputs, proj, xq_col_block, mkv, w_out, x, gain_ffn, w_router_t, n_tok, gdn, tm=512):
    M, D = x.shape
    tm = min(tm, n_tok // 2)
    assert n_tok % (2 * tm) == 0
    blocks_prompt = n_tok // tm
    blocks_sample = blocks_prompt // 2
    n_mem = mkv.shape[1]

    def mem_batch(i):
        return jnp.where(i < blocks_prompt, 0, 1 + (i - blocks_prompt) // blocks_sample)

    row = lambda width, cb=0: pl.BlockSpec((tm, width), lambda i: (i, cb))
    full = lambda a: pl.BlockSpec(a.shape, lambda i: (0,) * a.ndim)
    if gdn:
        o_f, o_b, out_gain = mix_inputs
        og = out_gain.reshape(1, HEAD_DIM).astype(F32)
        ins = [o_f, o_b, proj, og]
        specs = [row(MIX_DIM), row(MIX_DIM), row(MIX_DIM, 3), full(og)]
    else:
        ins = [mix_inputs[0]]
        specs = [row(MIX_DIM)]
    gf = gain_ffn.reshape(1, D).astype(F32)
    ins += [proj, mkv, w_out, x, gf, w_router_t]
    specs += [
        row(MEM_DIM, xq_col_block),
        pl.BlockSpec((None, n_mem, 2 * MEM_DIM), lambda i: (mem_batch(i), 0, 0)),
        full(w_out), row(D), full(gf), full(w_router_t),
    ]
    return pl.pallas_call(
        functools.partial(_post_mixer_kernel, gdn=gdn),
        grid=(M // tm,),
        in_specs=specs,
        out_specs=[row(D), row(D // 2), pl.BlockSpec((N_EXPERTS, tm), lambda i: (0, i))],
        out_shape=[jax.ShapeDtypeStruct((M, D), F32), jax.ShapeDtypeStruct((M, D // 2), jnp.uint32),
                   jax.ShapeDtypeStruct((N_EXPERTS, M), F32)],
        compiler_params=_params("parallel"),
        name="post_mixer_gdn" if gdn else "post_mixer_na",
    )(*ins)


def _expert_ffn_kernel(x_ref, gate_ref, wg_ref, wu_ref, wd_ref, o_ref, h_ref, wgb_ref, wub_ref, wdb_ref, xb_ref,
                       *, n_f, tf, rc):
    s = pl.program_id(1)
    cap = x_ref.shape[0]
    half = x_ref.shape[1]

    @pl.when(s == 0)
    def _():
        def body(c, carry):
            r = pl.multiple_of(c * rc, rc)
            w = x_ref[pl.ds(r, rc), :]
            hi = lax.bitcast_convert_type(w & jnp.uint32(0xFFFF0000), F32)
            lo = lax.bitcast_convert_type(w << 16, F32)
            xb_ref[pl.ds(r, rc), 0:half] = hi.astype(BF16)
            xb_ref[pl.ds(r, rc), half:2 * half] = lo.astype(BF16)
            return carry
        lax.fori_loop(0, cap // rc, body, 0)

    @pl.when(s < n_f)
    def _():
        wgb_ref[...] = wg_ref[...].astype(BF16)
        wub_ref[...] = wu_ref[...].astype(BF16)

        def body(c, carry):
            r = pl.multiple_of(c * rc, rc)
            xc = xb_ref[pl.ds(r, rc), :]
            g = _dot(xc, wgb_ref[...])
            u = _dot(xc, wub_ref[...])
            h_ref[s, pl.ds(r, rc), :] = (g * jax.nn.sigmoid(g) * u).astype(BF16)
            return carry
        lax.fori_loop(0, cap // rc, body, 0)

    @pl.when(s >= n_f)
    def _():
        wdb_ref[...] = wd_ref[...].astype(BF16)

        def body(c, carry):
            r = pl.multiple_of(c * rc, rc)
            y = _dot(h_ref[0, pl.ds(r, rc), :], wdb_ref[0:tf, :])
            for f in range(1, n_f):
                y = y + _dot(h_ref[f, pl.ds(r, rc), :], wdb_ref[f * tf:(f + 1) * tf, :])
            o_ref[pl.ds(r, rc), :] = y * gate_ref[pl.ds(r, rc), :]
            return carry
        lax.fori_loop(0, cap // rc, body, 0)


def expert_ffn(xe, gate, w_gate_up, w_down, layer, tf=256, td=256):
    G, cap, half = xe.shape
    D = 2 * half
    _, E, _, F2 = w_gate_up.shape
    F = F2 // 2
    tf = min(tf, F)
    td = min(td, D)
    n_f = F // tf
    n_d = D // td
    rc = min(512, cap)
    assert cap % rc == 0
    fcol = lambda s: jnp.minimum(s, n_f - 1)
    dcol = lambda s: jnp.maximum(s - n_f, 0)
    return pl.pallas_call(
        functools.partial(_expert_ffn_kernel, n_f=n_f, tf=tf, rc=rc),
        grid=(G, n_f + n_d),
        in_specs=[
            pl.BlockSpec((None, cap, half), lambda g, s: (g, 0, 0)),
            pl.BlockSpec((None, cap, 1), lambda g, s: (g, 0, 0)),
            pl.BlockSpec((None, None, D, tf), lambda g, s: (layer, g % E, 0, fcol(s))),
            pl.BlockSpec((None, None, D, tf), lambda g, s: (layer, g % E, 0, n_f + fcol(s))),
            pl.BlockSpec((None, None, F, td), lambda g, s: (layer, g % E, 0, dcol(s))),
        ],
        out_specs=pl.BlockSpec((None, cap, td), lambda g, s: (g, 0, dcol(s))),
        out_shape=jax.ShapeDtypeStruct((G, cap, D), F32),
        scratch_shapes=[pltpu.VMEM((n_f, cap, tf), BF16), pltpu.VMEM((D, tf), BF16), pltpu.VMEM((D, tf), BF16),
                        pltpu.VMEM((F, td), BF16), pltpu.VMEM((cap, D), BF16)],
        compiler_params=_params("parallel", "arbitrary"),
        name="expert_ffn",
    )(xe, gate, w_gate_up, w_gate_up, w_down)


def _combine_kernel(tile_ref, blk_ref, flag_ref, tok_ref, ys_ref, x_ref, g_ref, o_ref, acc_ref, *, final):
    s = pl.program_id(0)
    flag = flag_ref[s]
    T = acc_ref.shape[0]
    B = ys_ref.shape[0]

    @pl.when((flag & 1) != 0)
    def _():
        acc_ref[...] = jnp.zeros_like(acc_ref)

    @pl.when((flag & 4) != 0)
    def _():
        tokens = lax.broadcasted_iota(jnp.int32, (T, B), 0) + tile_ref[s] * T
        onehot = jnp.where(tok_ref[...] == tokens, 1.0, 0.0).astype(BF16)
        ys = ys_ref[...]
        hi = ys.astype(BF16)
        lo = (ys - hi.astype(F32)).astype(BF16)
        acc_ref[...] += _dot(onehot, hi) + _dot(onehot, lo)

    @pl.when((flag & 2) != 0)
    def _():
        y = x_ref[...] + acc_ref[...]
        if final:
            y = y * lax.rsqrt(jnp.mean(y * y, axis=-1, keepdims=True) + EPS) * g_ref[...]
        o_ref[...] = y


def _combine_schedule(tok_sorted, n_tiles, T, B):
    n_blocks = tok_sorted.shape[0] // B
    bounds = jnp.searchsorted(tok_sorted, jnp.arange(n_tiles + 1, dtype=jnp.int32) * T, side='left').astype(jnp.int32)
    lo, hi = bounds[:-1], bounds[1:]
    first_b = jnp.minimum(lo // B, n_blocks - 1)
    last_b = jnp.where(hi > lo, (hi - 1) // B, first_b)
    n = last_b - first_b + 1
    start = jnp.cumsum(n) - n
    total = jnp.sum(n)
    s = jnp.arange(n_blocks + n_tiles, dtype=jnp.int32)
    tile = jnp.clip(jnp.searchsorted(start, s, side='right').astype(jnp.int32) - 1, 0, n_tiles - 1)
    valid = s < total
    off = s - start[tile]
    blk = jnp.where(valid, first_b[tile] + off, n_blocks - 1)
    flag = jnp.where(valid, 4 + (off == 0) + 2 * (off == n[tile] - 1), 0)
    tile = jnp.where(valid, tile, n_tiles - 1)
    return tile.astype(jnp.int32), blk.astype(jnp.int32), flag.astype(jnp.int32)


def combine(x, ys, tok_sorted, final_gain=None, T=COMBINE_TILE, B=COMBINE_TILE):
    M, D = x.shape
    L = ys.shape[0]
    T = min(T, M)
    B = min(B, L)
    assert M % T == 0 and L % B == 0
    n_tiles, n_blocks = M // T, L // B
    tile, blk, flag = _combine_schedule(tok_sorted, n_tiles, T, B)
    final = final_gain is not None
    gain = (final_gain if final else jnp.ones((D,), F32)).reshape(1, D).astype(F32)
    grid_spec = pltpu.PrefetchScalarGridSpec(
        num_scalar_prefetch=3,
        grid=(n_blocks + n_tiles,),
        in_specs=[
            pl.BlockSpec((None, 1, B), lambda s, tile, blk, flag: (blk[s], 0, 0)),
            pl.BlockSpec((B, D), lambda s, tile, blk, flag: (blk[s], 0)),
            pl.BlockSpec((T, D), lambda s, tile, blk, flag: (tile[s], 0)),
            pl.BlockSpec((1, D), lambda s, tile, blk, flag: (0, 0)),
        ],
        out_specs=pl.BlockSpec((T, D), lambda s, tile, blk, flag: (tile[s], 0)),
        scratch_shapes=[pltpu.VMEM((T, D), F32)],
    )
    return pl.pallas_call(
        functools.partial(_combine_kernel, final=final),
        grid_spec=grid_spec,
        out_shape=jax.ShapeDtypeStruct((M, D), F32),
        compiler_params=_params("arbitrary"),
        name="moe_combine",
    )(tile, blk, flag, tok_sorted.astype(jnp.int32).reshape(n_blocks, 1, B), ys, x, gain)


def _moe(x, h_packed, aff_t, w_gate_up, w_down, layer, n_tok, final_gain=None):
    E = aff_t.shape[0]
    D = x.shape[1]
    cap = CAPACITY_FACTOR * n_tok // E
    gates, idxs = [], []
    for t in range(2):
        gate, idx = lax.top_k(aff_t[:, t * n_tok:(t + 1) * n_tok], cap)
        gates.append(gate)
        idxs.append(idx + t * n_tok)
    gate = jnp.concatenate(gates, axis=0)
    tok = jnp.concatenate(idxs, axis=0).reshape(-1)
    xe = h_packed[tok].reshape(2 * E, cap, D // 2)
    ye = expert_ffn(xe, gate[..., None], w_gate_up, w_down, layer)
    order = jnp.argsort(tok).astype(jnp.int32)
    ys = ye.reshape(-1, D)[order]
    return combine(x, ys, tok[order], final_gain)


def _forward(x_prompt, x_sample, mem_prompt, mem_sample, norm_mix, na_w_in, na_rpb, gdn_w_in, gdn_conv,
             gdn_a_log, gdn_dt_bias, gdn_out_norm, norm_mem, w_mem_kv, w_out, norm_ffn, w_router,
             w_gate_up, w_down, norm_final):
    D = x_prompt.shape[-1]
    n_tok = x_prompt.shape[0] * x_prompt.shape[1]
    assert x_prompt.shape[0] == 1 and x_sample.shape[0] == 2 and x_sample.shape[1] * 2 == n_tok
    n_mem = mem_prompt.shape[1]
    depth = norm_mix.shape[0]
    x = jnp.concatenate([x_prompt.reshape(n_tok, D), x_sample.reshape(n_tok, D)], axis=0)
    mem = jnp.concatenate([mem_prompt.reshape(n_mem, D), mem_sample.reshape(2 * n_mem, D)], axis=0)

    c_qkvz = 4 * MIX_DIM
    n_gate = 2 * MIX_HEADS
    gdn_cols = c_qkvz + MEM_DIM + 2 * n_gate
    gdn_cols_padded = -(-gdn_cols // 512) * 512

    for i in range(depth):
        j = i // 2
        if i % 2 == 0:
            w_in = na_w_in[j].astype(BF16)
            proj = rms_matmul(x, norm_mix[i], w_in, BF16)
            mix = neighbourhood_attention(proj, _na_bias_table(na_rpb[j]), n_tok)
            mix_inputs = (mix,)
            xq_block = 3 * MIX_DIM // MEM_DIM
        else:
            w = gdn_w_in[j]
            w_in = jnp.concatenate(
                [w[:, :c_qkvz], w[:, c_qkvz + 2 * n_gate:], w[:, c_qkvz:c_qkvz + 2 * n_gate],
                 jnp.zeros((D, gdn_cols_padded - gdn_cols), w.dtype)], axis=1).astype(BF16)
            proj = rms_matmul(x, norm_mix[i], w_in, F32)
            qkv = gdn_prep(proj, gdn_conv[j], n_tok)
            g0 = c_qkvz + MEM_DIM
            b_t = proj[:, g0:g0 + n_gate].T
            a_t = proj[:, g0 + n_gate:g0 + 2 * n_gate].T
            o_f, o_b = gated_delta(qkv, a_t, b_t, gdn_a_log[j], gdn_dt_bias[j], n_tok)
            mix_inputs = (o_f, o_b, gdn_out_norm[j])
            xq_block = c_qkvz // MEM_DIM
        mkv = rms_matmul(mem, norm_mem[i], w_mem_kv[i].astype(BF16), BF16, tm=n_mem)
        mkv = mkv.reshape(3, n_mem, 2 * MEM_DIM)
        x, h, aff_t = post_mixer(mix_inputs, proj, xq_block, mkv, w_out[i].astype(BF16), x, norm_ffn[i],
                                 w_router[i].T.astype(F32), n_tok, gdn=(i % 2 == 1))
        x = _moe(x, h, aff_t, w_gate_up, w_down, i, n_tok, final_gain=norm_final if i == depth - 1 else None)

    y = x
    return y[:n_tok].reshape(x_prompt.shape), y[n_tok:].reshape(x_sample.shape)


def kernel(x_prompt, x_sample, mem_prompt, mem_sample, norm_mix, na_w_in, na_rpb, gdn_w_in, gdn_conv, gdn_a_log, gdn_dt_bias, gdn_out_norm, norm_mem, w_mem_kv, w_out, norm_ffn, w_router, w_gate_up, w_down, norm_final):
    return _forward(x_prompt, x_sample, mem_prompt, mem_sample, norm_mix, na_w_in, na_rpb, gdn_w_in, gdn_conv,
                    gdn_a_log, gdn_dt_bias, gdn_out_norm, norm_mem, w_mem_kv, w_out, norm_ffn, w_router,
                    w_gate_up, w_down, norm_final)
```

```python
import functools
import math

import jax
import jax.numpy as jnp
import numpy as np
from jax import lax
from jax.experimental import pallas as pl
from jax.experimental.pallas import tpu as pltpu

F32 = jnp.float32
BF16 = jnp.bfloat16

HEAD_DIM = 128
MIX_HEADS = 12
MIX_DIM = MIX_HEADS * HEAD_DIM
MEM_HEADS = 4
MEM_DIM = MEM_HEADS * HEAD_DIM
GRID_W = 64
WIN_R = 8
WIN_C = 16
CONV_W = 5
CHUNK = 64
N_EXPERTS = 16
CAPACITY_FACTOR = 2
EPS = 1e-6
NEG_BIG = -1e30

V7X_VMEM_LIMIT_BYTES = 56 * 1024 * 1024
GDN_GROUP = 4 * CHUNK
GDN_HEADS_PER_STEP = 3
NA_ROWS_PER_ITER = 8
COMBINE_TILE = 256


def _params(*sem):
    return pltpu.CompilerParams(dimension_semantics=sem, vmem_limit_bytes=V7X_VMEM_LIMIT_BYTES)


def _dot(a, b):
    return jnp.dot(a, b, preferred_element_type=F32)


def _dot_nt(a, b):
    return lax.dot_general(a, b, (((1,), (1,)), ((), ())), preferred_element_type=F32)


def _dot_tn(a, b):
    return lax.dot_general(a, b, (((0,), (0,)), ((), ())), preferred_element_type=F32)


def _rms_matmul_kernel(x_ref, g_ref, w_ref, o_ref, xn_ref, *, tm, rc):
    @pl.when(pl.program_id(1) == 0)
    def _():
        def body(c, carry):
            r = pl.multiple_of(c * rc, rc)
            x = x_ref[pl.ds(r, rc), :]
            ms = jnp.mean(x * x, axis=-1, keepdims=True)
            xn_ref[pl.ds(r, rc), :] = (x * lax.rsqrt(ms + EPS) * g_ref[...]).astype(BF16)
            return carry
        lax.fori_loop(0, tm // rc, body, 0)

    o_ref[...] = _dot(xn_ref[...], w_ref[...]).astype(o_ref.dtype)


def rms_matmul(x, gain, w, out_dtype, tm=1024, tn=512):
    M, D = x.shape
    N = w.shape[1]
    tm = min(tm, M)
    tn = min(tn, N)
    assert M % tm == 0 and N % tn == 0
    rc = min(128, tm)
    return pl.pallas_call(
        functools.partial(_rms_matmul_kernel, tm=tm, rc=rc),
        grid=(M // tm, N // tn),
        in_specs=[
            pl.BlockSpec((tm, D), lambda i, j: (i, 0)),
            pl.BlockSpec((1, D), lambda i, j: (0, 0)),
            pl.BlockSpec((D, tn), lambda i, j: (0, j)),
        ],
        out_specs=pl.BlockSpec((tm, tn), lambda i, j: (i, j)),
        out_shape=jax.ShapeDtypeStruct((M, N), out_dtype),
        scratch_shapes=[pltpu.VMEM((tm, D), BF16)],
        compiler_params=_params("parallel", "arbitrary"),
        name="rms_matmul",
    )(x, gain.reshape(1, D), w)


def _na_bias_table(rpb):
    col = np.arange(GRID_W)
    cs = np.clip(col - WIN_C // 2, 0, GRID_W - WIN_C)
    c2 = np.arange(GRID_W)
    inside = (c2[None, :] >= cs[:, None]) & (c2[None, :] < cs[:, None] + WIN_C)
    pad = GRID_W - WIN_C
    padded = jnp.pad(rpb.astype(F32), ((0, 0), (0, 0), (pad, pad)))
    by_col = jnp.stack([padded[:, :, GRID_W - 1 - c:2 * GRID_W - 1 - c] for c in range(GRID_W)], axis=2)
    tab = jnp.stack([by_col[:, t:t + WIN_R] for t in range(WIN_R)], axis=1)
    tab = jnp.where(jnp.asarray(inside)[None, None, None], tab, NEG_BIG)
    tab = jnp.transpose(tab, (0, 1, 3, 2, 4))
    return tab.reshape(rpb.shape[0], WIN_R, GRID_W, WIN_R * GRID_W)


def _na_kernel(q_ref, k_ref, v_ref, b_ref, o_ref, *, rows_prompt, rows_sample):
    s = pl.program_id(1)
    rows = jnp.where(s == 0, rows_prompt, rows_sample)
    base = jnp.where(s == 2, rows_sample, 0)
    scale = HEAD_DIM ** -0.5
    kw = WIN_R * GRID_W

    def one_row(r):
        rs = jnp.clip(r - WIN_R // 2, 0, rows - WIN_R)
        t = rs - r + (WIN_R - 1)
        q = q_ref[pl.ds(pl.multiple_of((base + r) * GRID_W, GRID_W), GRID_W), :]
        k0 = pl.multiple_of((base + rs) * GRID_W, GRID_W)
        k = k_ref[pl.ds(k0, kw), :]
        v = v_ref[pl.ds(k0, kw), :]
        sc = _dot_nt(q, k) * scale + b_ref[t]
        yield
        m = jnp.max(sc, axis=-1, keepdims=True)
        p = jnp.exp(sc - m)
        l = jnp.sum(p, axis=-1, keepdims=True)
        o = _dot(p.astype(BF16), v) / l
        yield
        o_ref[pl.ds(pl.multiple_of((base + r) * GRID_W, GRID_W), GRID_W), :] = o.astype(o_ref.dtype)

    def body(i, carry):
        active = [one_row(i * NA_ROWS_PER_ITER + u) for u in range(NA_ROWS_PER_ITER)]
        while active:
            active = [g for g in active if next(g, StopIteration) is not StopIteration]
        return carry

    lax.fori_loop(0, rows // NA_ROWS_PER_ITER, body, 0)


def neighbourhood_attention(proj, bias_tab, n_tok):
    M = proj.shape[0]
    assert M == 2 * n_tok
    rows_prompt = n_tok // GRID_W
    rows_sample = rows_prompt // 2
    assert rows_sample >= WIN_R and rows_sample % NA_ROWS_PER_ITER == 0
    blk = lambda off: pl.BlockSpec((n_tok, HEAD_DIM), lambda h, s: (jnp.minimum(s, 1), h + off))
    return pl.pallas_call(
        functools.partial(_na_kernel, rows_prompt=rows_prompt, rows_sample=rows_sample),
        grid=(MIX_HEADS, 3),
        in_specs=[
            blk(0), blk(MIX_HEADS), blk(2 * MIX_HEADS),
            pl.BlockSpec((None, WIN_R, GRID_W, WIN_R * GRID_W), lambda h, s: (h, 0, 0, 0)),
        ],
        out_specs=blk(0),
        out_shape=jax.ShapeDtypeStruct((M, MIX_DIM), BF16),
        compiler_params=_params("parallel", "arbitrary"),
        name="neighbourhood_attention",
    )(proj, proj, proj, bias_tab)


def _softplus(x):
    return jnp.maximum(x, 0.0) + jnp.log1p(jnp.exp(-jnp.abs(x)))


def _gdn_local(q, k, v, g_row, beta_row, reverse, store):
    G = q.shape[0]
    nchunk = G // CHUNK
    kk = _dot_nt(k, k)
    qk = _dot_nt(q, k)
    yield
    ri = lax.broadcasted_iota(jnp.int32, (G, G), 0)
    ci = lax.broadcasted_iota(jnp.int32, (G, G), 1)
    shift = int(math.log2(CHUNK))
    same = jnp.right_shift(ri, shift) == jnp.right_shift(ci, shift)
    if reverse:
        incl = same & (ci >= ri)
        strict = same & (ci > ri)
    else:
        incl = same & (ci <= ri)
        strict = same & (ci < ri)
    eye = ri == ci

    gc = jnp.sum(jnp.where(incl, g_row, 0.0), axis=1, keepdims=True)
    gl = jnp.sum(jnp.where(same, g_row, 0.0), axis=1, keepdims=True)
    beta = jnp.sum(jnp.where(eye, beta_row, 0.0), axis=1, keepdims=True)
    cm = jnp.broadcast_to(gc, (G, G))
    diff = cm - cm.T
    decay = jnp.where(incl, jnp.exp(jnp.where(incl, diff, 0.0)), 0.0)
    a_neg = jnp.where(strict, -(beta * kk * decay), 0.0)

    tm = jnp.where(eye, 1.0, 0.0) + a_neg
    pb = a_neg.astype(BF16)
    p = _dot(pb, pb)
    yield
    for it in range(int(math.log2(CHUNK)) - 1):
        pb = p.astype(BF16)
        tm = tm + _dot(tm.astype(BF16), pb)
        if it < int(math.log2(CHUNK)) - 2:
            p = _dot(pb, pb)
        yield

    egc = jnp.exp(gc)
    kf = k.astype(F32)
    rhs = jnp.concatenate([v.astype(F32) * beta, kf * (beta * egc)], axis=1)
    uw = _dot(tm.astype(BF16), rhs.astype(BF16))
    yield
    dh = q.shape[1]
    u = uw[:, :dh]
    w = uw[:, dh:].astype(BF16)
    qkd = (qk * decay).astype(BF16)
    q_dec = (q.astype(F32) * egc).astype(BF16)
    k_dec = (kf * jnp.exp(gl - gc)).astype(BF16)
    wq = jnp.concatenate(
        [t[c * CHUNK:(c + 1) * CHUNK] for c in range(nchunk) for t in (w, q_dec)], axis=0)
    qkd_blocks = jnp.concatenate(
        [qkd[c * CHUNK:(c + 1) * CHUNK, c * CHUNK:(c + 1) * CHUNK] for c in range(nchunk)], axis=0)
    store(u, wq, k_dec, qkd_blocks, jnp.broadcast_to(jnp.exp(gl), (G, dh)))


def _gdn_scan(u, wq, k_dec, qkd_blocks, egl, s_state, reverse, store):
    G = u.shape[0]
    nchunk = G // CHUNK
    outs = [None] * nchunk
    order = range(nchunk - 1, -1, -1) if reverse else range(nchunk)
    for c in order:
        lo, hi = c * CHUNK, (c + 1) * CHUNK
        r = _dot(wq[2 * lo:2 * hi], s_state.astype(BF16))
        yield
        vb = (u[lo:hi] - r[:CHUNK]).astype(BF16)
        outs[c] = r[CHUNK:] + _dot(qkd_blocks[lo:hi], vb)
        s_state = s_state * egl[lo:lo + 1, :] + _dot_tn(k_dec[lo:hi], vb)
        yield
    store(jnp.concatenate(outs, axis=0), s_state)


def _gdn_kernel(qf_ref, kf_ref, vf_ref, af_ref, bf_ref, qb_ref, kb_ref, vb_ref, ab_ref, bb_ref,
                alog_ref, dtb_ref, of_ref, ob_ref, s_ref, u_ref, wq_ref, kd_ref, qkd_ref, egl_ref,
                *, steps_prompt, steps_sample):
    hb = pl.program_id(0)
    t = pl.program_id(1)
    carried = (s_ref, u_ref, wq_ref, kd_ref, qkd_ref, egl_ref)

    @pl.when(t == 0)
    def _():
        for ref in carried:
            ref[...] = jnp.zeros_like(ref)

    prev = t - 1
    @pl.when((prev == 0) | (prev == steps_prompt) | (prev == steps_prompt + steps_sample))
    def _():
        s_ref[...] = jnp.zeros_like(s_ref)

    scans, locals_ = [], []
    for d, (q_ref, k_ref, v_ref, a_ref, b_ref, o_ref) in enumerate(
            ((qf_ref, kf_ref, vf_ref, af_ref, bf_ref, of_ref), (qb_ref, kb_ref, vb_ref, ab_ref, bb_ref, ob_ref))):
        for j in range(GDN_HEADS_PER_STEP):
            lanes = slice(j * HEAD_DIM, (j + 1) * HEAD_DIM)

            def store_scan(o, s_new, d=d, j=j, o_ref=o_ref, lanes=lanes):
                s_ref[d, j] = s_new
                o_ref[:, lanes] = o

            scans.append(_gdn_scan(u_ref[d, j], wq_ref[d, j], kd_ref[d, j], qkd_ref[d, j], egl_ref[d, j],
                                   s_ref[d, j], reverse=(d == 1), store=store_scan))

            def store_local(u, wq, k_dec, qkd_blocks, egl, d=d, j=j):
                u_ref[d, j] = u
                wq_ref[d, j] = wq
                kd_ref[d, j] = k_dec
                qkd_ref[d, j] = qkd_blocks
                egl_ref[d, j] = egl

            gate_row = d * MIX_HEADS + hb * GDN_HEADS_PER_STEP + j
            a_row = a_ref[pl.ds(gate_row, 1), :]
            b_row = b_ref[pl.ds(gate_row, 1), :]
            g_row = -jnp.exp(alog_ref[pl.ds(gate_row, 1), :]) * _softplus(a_row + dtb_ref[pl.ds(gate_row, 1), :])
            locals_.append(_gdn_local(q_ref[:, lanes], k_ref[:, lanes], v_ref[:, lanes], g_row,
                                      jax.nn.sigmoid(b_row), reverse=(d == 1), store=store_local))

    active = scans + locals_
    while active:
        active = [g for g in active if next(g, StopIteration) is not StopIteration]


def gated_delta(qkv, a_t, b_t, a_log, dt_bias, n_tok):
    M = qkv.shape[0]
    assert M == 2 * n_tok
    steps_prompt = n_tok // GDN_GROUP
    steps_sample = steps_prompt // 2
    assert steps_sample >= 1 and n_tok % (2 * GDN_GROUP) == 0
    n_steps = 2 * steps_prompt
    nh = GDN_HEADS_PER_STEP
    hw = nh * HEAD_DIM
    col_blocks = MIX_DIM // hw

    def bwd_block(t):
        in_prompt = t < steps_prompt
        in_s1 = t < steps_prompt + steps_sample
        start = jnp.where(in_prompt, 0, jnp.where(in_s1, steps_prompt, steps_prompt + steps_sample))
        length = jnp.where(in_prompt, steps_prompt, steps_sample)
        return start + length - 1 - (t - start)

    local = lambda t: jnp.minimum(t, n_steps - 1)
    scan = lambda t: jnp.maximum(t - 1, 0)
    tok_f = lambda off: pl.BlockSpec((GDN_GROUP, hw), lambda h, t: (local(t), h + off * col_blocks))
    tok_b = lambda off: pl.BlockSpec((GDN_GROUP, hw), lambda h, t: (bwd_block(local(t)), h + off * col_blocks))
    gate_f = pl.BlockSpec((2 * MIX_HEADS, GDN_GROUP), lambda h, t: (0, local(t)))
    gate_b = pl.BlockSpec((2 * MIX_HEADS, GDN_GROUP), lambda h, t: (0, bwd_block(local(t))))
    small = pl.BlockSpec((2 * MIX_HEADS, 1), lambda h, t: (0, 0))
    out_f = pl.BlockSpec((GDN_GROUP, hw), lambda h, t: (scan(t), h))
    out_b = pl.BlockSpec((GDN_GROUP, hw), lambda h, t: (bwd_block(scan(t)), h))
    out = jax.ShapeDtypeStruct((M, MIX_DIM), F32)
    per_unit = lambda shape, dtype: pltpu.VMEM((2, nh) + shape, dtype)
    return pl.pallas_call(
        functools.partial(_gdn_kernel, steps_prompt=steps_prompt, steps_sample=steps_sample),
        grid=(MIX_HEADS // nh, n_steps + 1),
        in_specs=[tok_f(0), tok_f(1), tok_f(2), gate_f, gate_f, tok_b(0), tok_b(1), tok_b(2), gate_b, gate_b,
                  small, small],
        out_specs=[out_f, out_b],
        out_shape=[out, out],
        scratch_shapes=[per_unit((HEAD_DIM, HEAD_DIM), F32), per_unit((GDN_GROUP, HEAD_DIM), F32),
                        per_unit((2 * GDN_GROUP, HEAD_DIM), BF16), per_unit((GDN_GROUP, HEAD_DIM), BF16),
                        per_unit((GDN_GROUP, CHUNK), BF16), per_unit((GDN_GROUP, HEAD_DIM), F32)],
        compiler_params=_params("parallel", "arbitrary"),
        name="gated_delta",
    )(qkv, qkv, qkv, a_t, b_t, qkv, qkv, qkv, a_t, b_t,
      a_log.reshape(2 * MIX_HEADS, 1).astype(F32), dt_bias.reshape(2 * MIX_HEADS, 1).astype(F32))


def _gdn_prep_kernel(prev_ref, x_ref, next_ref, w_ref, o_ref, *, blocks_prompt, blocks_sample):
    i = pl.program_id(0)
    part = pl.program_id(1)
    tb = x_ref.shape[0]
    s1 = blocks_prompt
    s2 = blocks_prompt + blocks_sample
    s3 = blocks_prompt + 2 * blocks_sample
    pmask = jnp.where((i == 0) | (i == s1) | (i == s2), 0.0, 1.0)
    nmask = jnp.where((i == s1 - 1) | (i == s2 - 1) | (i == s3 - 1), 0.0, 1.0)
    row = lax.broadcasted_iota(jnp.int32, (tb, HEAD_DIM), 0)
    qscale = jnp.where(part == 0, HEAD_DIM ** -0.5, 1.0)
    for h in range(MIX_HEADS):
        lanes = slice(h * HEAD_DIM, (h + 1) * HEAD_DIM)
        x = x_ref[:, lanes]
        pv = prev_ref[:, lanes] * pmask
        nx = next_ref[:, lanes] * nmask
        w = w_ref[:, lanes]
        xm1 = jnp.where(row == 0, pv[7:8], pltpu.roll(x, 1, axis=0))
        xm2 = jnp.where(row == 0, pv[6:7], jnp.where(row == 1, pv[7:8], pltpu.roll(x, 2, axis=0)))
        xp1 = jnp.where(row == tb - 1, nx[0:1], pltpu.roll(x, tb - 1, axis=0))
        xp2 = jnp.where(row == tb - 2, nx[0:1], jnp.where(row == tb - 1, nx[1:2], pltpu.roll(x, tb - 2, axis=0)))
        y = xm2 * w[0:1] + xm1 * w[1:2] + x * w[2:3] + xp1 * w[3:4] + xp2 * w[4:5]
        y = y * jax.nn.sigmoid(y)
        inv = lax.rsqrt(jnp.sum(y * y, axis=-1, keepdims=True) + EPS) * qscale
        o_ref[:, lanes] = (y * jnp.where(part == 2, 1.0, inv)).astype(o_ref.dtype)


def gdn_prep(proj, conv_w, n_tok, tb=256):
    M = proj.shape[0]
    assert CONV_W == 5 and M == 2 * n_tok
    tb = min(tb, n_tok // 2)
    assert n_tok % (2 * tb) == 0 and tb % 8 == 0
    blocks_prompt = n_tok // tb
    blocks_sample = blocks_prompt // 2
    r8 = tb // 8
    return pl.pallas_call(
        functools.partial(_gdn_prep_kernel, blocks_prompt=blocks_prompt, blocks_sample=blocks_sample),
        grid=(M // tb, 3),
        in_specs=[
            pl.BlockSpec((8, MIX_DIM), lambda i, j: (jnp.maximum(i * r8 - 1, 0), j)),
            pl.BlockSpec((tb, MIX_DIM), lambda i, j: (i, j)),
            pl.BlockSpec((8, MIX_DIM), lambda i, j: (jnp.minimum((i + 1) * r8, M // 8 - 1), j)),
            pl.BlockSpec((CONV_W, MIX_DIM), lambda i, j: (0, j)),
        ],
        out_specs=pl.BlockSpec((tb, MIX_DIM), lambda i, j: (i, j)),
        out_shape=jax.ShapeDtypeStruct((M, 3 * MIX_DIM), BF16),
        compiler_params=_params("parallel", "arbitrary"),
        name="gdn_prep",
    )(proj, proj, proj, conv_w.astype(F32))


def _post_mixer_kernel(*refs, gdn):
    if gdn:
        (of_ref, ob_ref, z_ref, og_ref, xq_ref, mkv_ref, wo_ref, x_ref, gf_ref, wr_ref,
         xo_ref, h_ref, aff_ref) = refs
    else:
        (mix_ref, xq_ref, mkv_ref, wo_ref, x_ref, gf_ref, wr_ref, xo_ref, h_ref, aff_ref) = refs
    scale = HEAD_DIM ** -0.5

    parts = []
    if gdn:
        for h in range(MIX_HEADS):
            lanes = slice(h * HEAD_DIM, (h + 1) * HEAD_DIM)
            o = of_ref[:, lanes] + ob_ref[:, lanes]
            o = o * lax.rsqrt(jnp.mean(o * o, axis=-1, keepdims=True) + EPS) * og_ref[...]
            z = z_ref[:, lanes]
            parts.append((o * (z * jax.nn.sigmoid(z))).astype(BF16))
    else:
        parts.append(mix_ref[...])

    for h in range(MEM_HEADS):
        q = xq_ref[:, h * HEAD_DIM:(h + 1) * HEAD_DIM].astype(BF16)
        k = mkv_ref[:, h * HEAD_DIM:(h + 1) * HEAD_DIM]
        v = mkv_ref[:, MEM_DIM + h * HEAD_DIM:MEM_DIM + (h + 1) * HEAD_DIM]
        sc = _dot_nt(q, k) * scale
        p = jnp.exp(sc - jnp.max(sc, axis=-1, keepdims=True))
        l = jnp.sum(p, axis=-1, keepdims=True)
        parts.append((_dot(p.astype(BF16), v) / l).astype(BF16))

    lhs = jnp.concatenate(parts, axis=1)
    x = x_ref[...] + _dot(lhs, wo_ref[...])
    xo_ref[...] = x
    hn = x * lax.rsqrt(jnp.mean(x * x, axis=-1, keepdims=True) + EPS) * gf_ref[...]
    bits = lax.bitcast_convert_type(hn.astype(BF16).astype(F32), jnp.uint32)
    half = bits.shape[1] // 2
    h_ref[...] = bits[:, :half] | (bits[:, half:] >> 16)
    logits = lax.dot_general(wr_ref[...], hn, (((1,), (1,)), ((), ())), preferred_element_type=F32,
                             precision=lax.Precision.HIGHEST)
    e = jnp.exp(logits - jnp.max(logits, axis=0, keepdims=True))
    aff_ref[...] = e / jnp.sum(e, axis=0, keepdims=True)


def post_mixer(mix_inputs, proj, xq_col_block, mkv, w_out, x, gain_ffn, w_router_t, n_tok, gdn, tm=512):
    M, D = x.shape
    tm = min(tm, n_tok // 2)
    assert n_tok % (2 * tm) == 0
    blocks_prompt = n_tok // tm
    blocks_sample = blocks_prompt // 2
    n_mem = mkv.shape[1]

    def mem_batch(i):
        return jnp.where(i < blocks_prompt, 0, 1 + (i - blocks_prompt) // blocks_sample)

    row = lambda width, cb=0: pl.BlockSpec((tm, width), lambda i: (i, cb))
    full = lambda a: pl.BlockSpec(a.shape, lambda i: (0,) * a.ndim)
    if gdn:
        o_f, o_b, out_gain = mix_inputs
        og = out_gain.reshape(1, HEAD_DIM).astype(F32)
        ins = [o_f, o_b, proj, og]
        specs = [row(MIX_DIM), row(MIX_DIM), row(MIX_DIM, 3), full(og)]
    else:
        ins = [mix_inputs[0]]
        specs = [row(MIX_DIM)]
    gf = gain_ffn.reshape(1, D).astype(F32)
    ins += [proj, mkv, w_out, x, gf, w_router_t]
    specs += [
        row(MEM_DIM, xq_col_block),
        pl.BlockSpec((None, n_mem, 2 * MEM_DIM), lambda i: (mem_batch(i), 0, 0)),
        full(w_out), row(D), full(gf), full(w_router_t),
    ]
    return pl.pallas_call(
        functools.partial(_post_mixer_kernel, gdn=gdn),
        grid=(M // tm,),
        in_specs=specs,
        out_specs=[row(D), row(D // 2), pl.BlockSpec((N_EXPERTS, tm), lambda i: (0, i))],
        out_shape=[jax.ShapeDtypeStruct((M, D), F32), jax.ShapeDtypeStruct((M, D // 2), jnp.uint32),
                   jax.ShapeDtypeStruct((N_EXPERTS, M), F32)],
        compiler_params=_params("parallel"),
        name="post_mixer_gdn" if gdn else "post_mixer_na",
    )(*ins)


def _expert_ffn_kernel(x_ref, gate_ref, wg_ref, wu_ref, wd_ref, o_ref, h_ref, wgb_ref, wub_ref, wdb_ref, xb_ref,
                       *, n_f, tf, rc):
    s = pl.program_id(1)
    cap = x_ref.shape[0]
    half = x_ref.shape[1]

    @pl.when(s == 0)
    def _():
        def body(c, carry):
            r = pl.multiple_of(c * rc, rc)
            w = x_ref[pl.ds(r, rc), :]
            hi = lax.bitcast_convert_type(w & jnp.uint32(0xFFFF0000), F32)
            lo = lax.bitcast_convert_type(w << 16, F32)
            xb_ref[pl.ds(r, rc), 0:half] = hi.astype(BF16)
            xb_ref[pl.ds(r, rc), half:2 * half] = lo.astype(BF16)
            return carry
        lax.fori_loop(0, cap // rc, body, 0)

    @pl.when(s < n_f)
    def _():
        wgb_ref[...] = wg_ref[...].astype(BF16)
        wub_ref[...] = wu_ref[...].astype(BF16)

        def body(c, carry):
            r = pl.multiple_of(c * rc, rc)
            xc = xb_ref[pl.ds(r, rc), :]
            g = _dot(xc, wgb_ref[...])
            u = _dot(xc, wub_ref[...])
            h_ref[s, pl.ds(r, rc), :] = (g * jax.nn.sigmoid(g) * u).astype(BF16)
            return carry
        lax.fori_loop(0, cap // rc, body, 0)

    @pl.when(s >= n_f)
    def _():
        wdb_ref[...] = wd_ref[...].astype(BF16)

        def body(c, carry):
            r = pl.multiple_of(c * rc, rc)
            y = _dot(h_ref[0, pl.ds(r, rc), :], wdb_ref[0:tf, :])
            for f in range(1, n_f):
                y = y + _dot(h_ref[f, pl.ds(r, rc), :], wdb_ref[f * tf:(f + 1) * tf, :])
            o_ref[pl.ds(r, rc), :] = y * gate_ref[pl.ds(r, rc), :]
            return carry
        lax.fori_loop(0, cap // rc, body, 0)


def expert_ffn(xe, gate, w_gate_up, w_down, layer, tf=256, td=256):
    G, cap, half = xe.shape
    D = 2 * half
    _, E, _, F2 = w_gate_up.shape
    F = F2 // 2
    tf = min(tf, F)
    td = min(td, D)
    n_f = F // tf
    n_d = D // td
    rc = min(512, cap)
    assert cap % rc == 0
    fcol = lambda s: jnp.minimum(s, n_f - 1)
    dcol = lambda s: jnp.maximum(s - n_f, 0)
    return pl.pallas_call(
        functools.partial(_expert_ffn_kernel, n_f=n_f, tf=tf, rc=rc),
        grid=(G, n_f + n_d),
        in_specs=[
            pl.BlockSpec((None, cap, half), lambda g, s: (g, 0, 0)),
            pl.BlockSpec((None, cap, 1), lambda g, s: (g, 0, 0)),
            pl.BlockSpec((None, None, D, tf), lambda g, s: (layer, g % E, 0, fcol(s))),
            pl.BlockSpec((None, None, D, tf), lambda g, s: (layer, g % E, 0, n_f + fcol(s))),
            pl.BlockSpec((None, None, F, td), lambda g, s: (layer, g % E, 0, dcol(s))),
        ],
        out_specs=pl.BlockSpec((None, cap, td), lambda g, s: (g, 0, dcol(s))),
        out_shape=jax.ShapeDtypeStruct((G, cap, D), F32),
        scratch_shapes=[pltpu.VMEM((n_f, cap, tf), BF16), pltpu.VMEM((D, tf), BF16), pltpu.VMEM((D, tf), BF16),
                        pltpu.VMEM((F, td), BF16), pltpu.VMEM((cap, D), BF16)],
        compiler_params=_params("parallel", "arbitrary"),
        name="expert_ffn",
    )(xe, gate, w_gate_up, w_gate_up, w_down)


def _combine_kernel(tile_ref, blk_ref, flag_ref, tok_ref, ys_ref, x_ref, g_ref, o_ref, acc_ref, *, final):
    s = pl.program_id(0)
    flag = flag_ref[s]
    T = acc_ref.shape[0]
    B = ys_ref.shape[0]

    @pl.when((flag & 1) != 0)
    def _():
        acc_ref[...] = jnp.zeros_like(acc_ref)

    @pl.when((flag & 4) != 0)
    def _():
        tokens = lax.broadcasted_iota(jnp.int32, (T, B), 0) + tile_ref[s] * T
        onehot = jnp.where(tok_ref[...] == tokens, 1.0, 0.0).astype(BF16)
        ys = ys_ref[...]
        hi = ys.astype(BF16)
        lo = (ys - hi.astype(F32)).astype(BF16)
        acc_ref[...] += _dot(onehot, hi) + _dot(onehot, lo)

    @pl.when((flag & 2) != 0)
    def _():
        y = x_ref[...] + acc_ref[...]
        if final:
            y = y * lax.rsqrt(jnp.mean(y * y, axis=-1, keepdims=True) + EPS) * g_ref[...]
        o_ref[...] = y


def _combine_schedule(tok_sorted, n_tiles, T, B):
    n_blocks = tok_sorted.shape[0] // B
    bounds = jnp.searchsorted(tok_sorted, jnp.arange(n_tiles + 1, dtype=jnp.int32) * T, side='left').astype(jnp.int32)
    lo, hi = bounds[:-1], bounds[1:]
    first_b = jnp.minimum(lo // B, n_blocks - 1)
    last_b = jnp.where(hi > lo, (hi - 1) // B, first_b)
    n = last_b - first_b + 1
    start = jnp.cumsum(n) - n
    total = jnp.sum(n)
    s = jnp.arange(n_blocks + n_tiles, dtype=jnp.int32)
    tile = jnp.clip(jnp.searchsorted(start, s, side='right').astype(jnp.int32) - 1, 0, n_tiles - 1)
    valid = s < total
    off = s - start[tile]
    blk = jnp.where(valid, first_b[tile] + off, n_blocks - 1)
    flag = jnp.where(valid, 4 + (off == 0) + 2 * (off == n[tile] - 1), 0)
    tile = jnp.where(valid, tile, n_tiles - 1)
    return tile.astype(jnp.int32), blk.astype(jnp.int32), flag.astype(jnp.int32)


def combine(x, ys, tok_sorted, final_gain=None, T=COMBINE_TILE, B=COMBINE_TILE):
    M, D = x.shape
    L = ys.shape[0]
    T = min(T, M)
    B = min(B, L)
    assert M % T == 0 and L % B == 0
    n_tiles, n_blocks = M // T, L // B
    tile, blk, flag = _combine_schedule(tok_sorted, n_tiles, T, B)
    final = final_gain is not None
    gain = (final_gain if final else jnp.ones((D,), F32)).reshape(1, D).astype(F32)
    grid_spec = pltpu.PrefetchScalarGridSpec(
        num_scalar_prefetch=3,
        grid=(n_blocks + n_tiles,),
        in_specs=[
            pl.BlockSpec((None, 1, B), lambda s, tile, blk, flag: (blk[s], 0, 0)),
            pl.BlockSpec((B, D), lambda s, tile, blk, flag: (blk[s], 0)),
            pl.BlockSpec((T, D), lambda s, tile, blk, flag: (tile[s], 0)),
            pl.BlockSpec((1, D), lambda s, tile, blk, flag: (0, 0)),
        ],
        out_specs=pl.BlockSpec((T, D), lambda s, tile, blk, flag: (tile[s], 0)),
        scratch_shapes=[pltpu.VMEM((T, D), F32)],
    )
    return pl.pallas_call(
        functools.partial(_combine_kernel, final=final),
        grid_spec=grid_spec,
        out_shape=jax.ShapeDtypeStruct((M, D), F32),
        compiler_params=_params("arbitrary"),
        name="moe_combine",
    )(tile, blk, flag, tok_sorted.astype(jnp.int32).reshape(n_blocks, 1, B), ys, x, gain)


def _moe(x, h_packed, aff_t, w_gate_up, w_down, layer, n_tok, final_gain=None):
    E = aff_t.shape[0]
    D = x.shape[1]
    cap = CAPACITY_FACTOR * n_tok // E
    gates, idxs = [], []
    for t in range(2):
        gate, idx = lax.top_k(aff_t[:, t * n_tok:(t + 1) * n_tok], cap)
        gates.append(gate)
        idxs.append(idx + t * n_tok)
    gate = jnp.concatenate(gates, axis=0)
    tok = jnp.concatenate(idxs, axis=0).reshape(-1)
    xe = h_packed[tok].reshape(2 * E, cap, D // 2)
    ye = expert_ffn(xe, gate[..., None], w_gate_up, w_down, layer)
    order = jnp.argsort(tok).astype(jnp.int32)
    ys = ye.reshape(-1, D)[order]
    return combine(x, ys, tok[order], final_gain)


def _forward(x_prompt, x_sample, mem_prompt, mem_sample, norm_mix, na_w_in, na_rpb, gdn_w_in, gdn_conv,
             gdn_a_log, gdn_dt_bias, gdn_out_norm, norm_mem, w_mem_kv, w_out, norm_ffn, w_router,
             w_gate_up, w_down, norm_final):
    D = x_prompt.shape[-1]
    n_tok = x_prompt.shape[0] * x_prompt.shape[1]
    assert x_prompt.shape[0] == 1 and x_sample.shape[0] == 2 and x_sample.shape[1] * 2 == n_tok
    n_mem = mem_prompt.shape[1]
    depth = norm_mix.shape[0]
    x = jnp.concatenate([x_prompt.reshape(n_tok, D), x_sample.reshape(n_tok, D)], axis=0)
    mem = jnp.concatenate([mem_prompt.reshape(n_mem, D), mem_sample.reshape(2 * n_mem, D)], axis=0)

    c_qkvz = 4 * MIX_DIM
    n_gate = 2 * MIX_HEADS
    gdn_cols = c_qkvz + MEM_DIM + 2 * n_gate
    gdn_cols_padded = -(-gdn_cols // 512) * 512

    for i in range(depth):
        j = i // 2
        if i % 2 == 0:
            w_in = na_w_in[j].astype(BF16)
            proj = rms_matmul(x, norm_mix[i], w_in, BF16)
            mix = neighbourhood_attention(proj, _na_bias_table(na_rpb[j]), n_tok)
            mix_inputs = (mix,)
            xq_block = 3 * MIX_DIM // MEM_DIM
        else:
            w = gdn_w_in[j]
            w_in = jnp.concatenate(
                [w[:, :c_qkvz], w[:, c_qkvz + 2 * n_gate:], w[:, c_qkvz:c_qkvz + 2 * n_gate],
                 jnp.zeros((D, gdn_cols_padded - gdn_cols), w.dtype)], axis=1).astype(BF16)
            proj = rms_matmul(x, norm_mix[i], w_in, F32)
            qkv = gdn_prep(proj, gdn_conv[j], n_tok)
            g0 = c_qkvz + MEM_DIM
            b_t = proj[:, g0:g0 + n_gate].T
            a_t = proj[:, g0 + n_gate:g0 + 2 * n_gate].T
            o_f, o_b = gated_delta(qkv, a_t, b_t, gdn_a_log[j], gdn_dt_bias[j], n_tok)
            mix_inputs = (o_f, o_b, gdn_out_norm[j])
            xq_block = c_qkvz // MEM_DIM
        mkv = rms_matmul(mem, norm_mem[i], w_mem_kv[i].astype(BF16), BF16, tm=n_mem)
        mkv = mkv.reshape(3, n_mem, 2 * MEM_DIM)
        x, h, aff_t = post_mixer(mix_inputs, proj, xq_block, mkv, w_out[i].astype(BF16), x, norm_ffn[i],
                                 w_router[i].T.astype(F32), n_tok, gdn=(i % 2 == 1))
        x = _moe(x, h, aff_t, w_gate_up, w_down, i, n_tok, final_gain=norm_final if i == depth - 1 else None)

    y = x
    return y[:n_tok].reshape(x_prompt.shape), y[n_tok:].reshape(x_sample.shape)


def kernel(x_prompt, x_sample, mem_prompt, mem_sample, norm_mix, na_w_in, na_rpb, gdn_w_in, gdn_conv, gdn_a_log, gdn_dt_bias, gdn_out_norm, norm_mem, w_mem_kv, w_out, norm_ffn, w_router, w_gate_up, w_down, norm_final):
    return _forward(x_prompt, x_sample, mem_prompt, mem_sample, norm_mix, na_w_in, na_rpb, gdn_w_in, gdn_conv,
                    gdn_a_log, gdn_dt_bias, gdn_out_norm, norm_mem, w_mem_kv, w_out, norm_ffn, w_router,
                    w_gate_up, w_down, norm_final)
```
